```python
import jax, jax.numpy as jnp
from jax import lax
import numpy as np

D_MODEL = 1024
BATCH = 4
SEQ = 4096
DEPTH = 1
DEC_BATCH = 1
DEC_SEQ = 16384
PAST_LEN = 128

N_HEADS = 8
N_KV_HEADS = 2
HEAD_DIM = 64
Q_PER_KV = N_HEADS // N_KV_HEADS
ATTN_WIDTH = N_HEADS * HEAD_DIM
KV_WIDTH = N_KV_HEADS * HEAD_DIM
GMLP_GROUPS = 8
GMLP_GROUP_DIM = 64
GMLP_WIDTH = GMLP_GROUPS * GMLP_GROUP_DIM
CHUNK = 128
Q_BLOCK = 128
GRID_W = 64
ROPE_THETA = 10000.0
ROPE_AXIS_FREQS = HEAD_DIM // 4
D_FF = 2816
PLE_DIM = 256
EPS = 1e-6
IN_SPLITS = (ATTN_WIDTH, KV_WIDTH, KV_WIDTH, GMLP_WIDTH, GMLP_WIDTH, D_MODEL, D_MODEL)
IN_WIDTH = 3840
IN_OFFSETS = (512, 640, 768, 1280, 1792, 2816)

kernel_name = 'hybrid_gqa_gmlp_macaron_encoder'


def rms_norm(x, g):
    xf = x.astype(jnp.float32)
    y = xf * lax.rsqrt(jnp.mean(xf * xf, axis=-1, keepdims=True) + EPS)
    return (y * g.astype(jnp.float32)).astype(x.dtype)


def swiglu(x, w_gu, w_down):
    a, b = jnp.split(x @ w_gu, 2, axis=-1)
    return (jax.nn.silu(a) * b) @ w_down


def axial_rope_tables(n_tok, dtype):
    rows = n_tok // GRID_W
    row = jnp.repeat(jnp.arange(rows, dtype=jnp.float32), GRID_W)
    col = jnp.tile(jnp.arange(GRID_W, dtype=jnp.float32), rows)
    inv = jnp.power(jnp.float32(ROPE_THETA), -jnp.arange(ROPE_AXIS_FREQS, dtype=jnp.float32) / ROPE_AXIS_FREQS)
    ang = jnp.stack([row[:, None] * inv, col[:, None] * inv], axis=1)
    return jnp.cos(ang).astype(dtype), jnp.sin(ang).astype(dtype)


def apply_axial_rope(x, cos, sin):
    B, S, H, _ = x.shape
    xr = x.reshape(B, S, H, 2, 2, ROPE_AXIS_FREQS)
    x1 = xr[..., 0, :]
    x2 = xr[..., 1, :]
    c = cos[None, :, None]
    s = sin[None, :, None]
    out = jnp.stack([x1 * c - x2 * s, x2 * c + x1 * s], axis=-2)
    return out.reshape(B, S, H, HEAD_DIM)


def blockwise_attention(q, k, v):
    B, S = q.shape[0], q.shape[1]
    nb = S // Q_BLOCK
    qb = q.reshape(B, nb, Q_BLOCK, N_KV_HEADS, Q_PER_KV, HEAD_DIM).transpose(1, 0, 2, 3, 4, 5)
    scale = HEAD_DIM ** -0.5

    def one_block(q_blk):
        s = jnp.einsum('bqkgd,bskd->bkgqs', q_blk, k, preferred_element_type=jnp.float32) * scale
        pr = jax.nn.softmax(s, axis=-1)
        return jnp.einsum('bkgqs,bskd->bqkgd', pr.astype(v.dtype), v)

    o = lax.map(one_block, qb)
    return o.transpose(1, 0, 2, 3, 4, 5).reshape(B, S, ATTN_WIDTH)


def spatial_gating(u, v, g_v, w_s, b_s):
    B, S, _ = u.shape
    n = S // CHUNK
    u = jax.nn.gelu(u, approximate=False)
    v = rms_norm(jax.nn.gelu(v, approximate=False), g_v)
    vc = v.reshape(B, n, CHUNK, GMLP_GROUPS, GMLP_GROUP_DIM)
    mixed = jnp.einsum('gpq,bnqgc->bnpgc', w_s, vc) + b_s.T[None, None, :, :, None]
    return u * mixed.reshape(B, S, GMLP_WIDTH)


def encoder_trunk(x, p, g_ffn1, w_ffn1_gu, w_ffn1_down, g_mix, w_in, g_q, g_k, g_gmlp_v, w_spatial, b_spatial,
                  w_branch_attn, w_branch_gmlp, w_out, g_ffn2, w_ffn2_gu, w_ffn2_down, g_ple, w_ple_gate, w_ple, g_final):
    B, S, _ = x.shape
    cos, sin = axial_rope_tables(S, x.dtype)
    for i in range(DEPTH):
        x = x + 0.5 * swiglu(rms_norm(x, g_ffn1[i]), w_ffn1_gu[i], w_ffn1_down[i])
        h = rms_norm(x, g_mix[i])
        q, k, v, gu, gv, gate_a, gate_b = jnp.split(h @ w_in[i], IN_OFFSETS, axis=-1)
        q = apply_axial_rope(rms_norm(q.reshape(B, S, N_HEADS, HEAD_DIM), g_q[i]), cos, sin)
        k = apply_axial_rope(rms_norm(k.reshape(B, S, N_KV_HEADS, HEAD_DIM), g_k[i]), cos, sin)
        v = v.reshape(B, S, N_KV_HEADS, HEAD_DIM)
        a = blockwise_attention(q, k, v)
        sg = spatial_gating(gu, gv, g_gmlp_v[i], w_spatial[i], b_spatial[i])
        merged = (jax.nn.sigmoid(gate_a) * (a @ w_branch_attn[i])
                  + jax.nn.sigmoid(gate_b) * (sg @ w_branch_gmlp[i]))
        x = x + merged @ w_out[i]
        x = x + 0.5 * swiglu(rms_norm(x, g_ffn2[i]), w_ffn2_gu[i], w_ffn2_down[i])
        x = x + jax.nn.sigmoid(rms_norm(x, g_ple[i]) @ w_ple_gate[i]) * (p[i] @ w_ple[i])
    return rms_norm(x, g_final)


def setup_inputs(seed: int = 0) -> dict:
    key = jax.random.key(seed)
    ks = jax.random.split(key, 32)
    f32 = jnp.float32

    def nrm(k, shape, fan_in):
        return jax.random.normal(k, shape, f32) * (fan_in ** -0.5)

    def gain(k, shape):
        return 1.0 + 0.02 * jax.random.normal(k, shape, f32)

    return {
        'x_prompt': jax.random.normal(ks[0], (BATCH, SEQ, D_MODEL), f32),
        'x_sample': jax.random.normal(ks[1], (DEC_BATCH, DEC_SEQ, D_MODEL), f32),
        'p_prompt': jax.random.normal(ks[2], (DEPTH, BATCH, SEQ, PLE_DIM), f32),
        'p_sample': jax.random.normal(ks[3], (DEPTH, DEC_BATCH, DEC_SEQ, PLE_DIM), f32),
        'g_ffn1': gain(ks[4], (DEPTH, D_MODEL)),
        'w_ffn1_gu': nrm(ks[5], (DEPTH, D_MODEL, 2 * D_FF), D_MODEL),
        'w_ffn1_down': nrm(ks[6], (DEPTH, D_FF, D_MODEL), D_FF),
        'g_mix': gain(ks[7], (DEPTH, D_MODEL)),
        'w_in': nrm(ks[8], (DEPTH, D_MODEL, IN_WIDTH), D_MODEL),
        'g_q': gain(ks[9], (DEPTH, HEAD_DIM)),
        'g_k': gain(ks[10], (DEPTH, HEAD_DIM)),
        'g_gmlp_v': gain(ks[11], (DEPTH, GMLP_WIDTH)),
        'w_spatial': nrm(ks[12], (DEPTH, GMLP_GROUPS, CHUNK, CHUNK), CHUNK),
        'b_spatial': 1.0 + 0.02 * jax.random.normal(ks[13], (DEPTH, GMLP_GROUPS, CHUNK), f32),
        'w_branch_attn': nrm(ks[14], (DEPTH, ATTN_WIDTH, D_MODEL), ATTN_WIDTH),
        'w_branch_gmlp': nrm(ks[15], (DEPTH, GMLP_WIDTH, D_MODEL), GMLP_WIDTH),
        'w_out': nrm(ks[16], (DEPTH, D_MODEL, D_MODEL), D_MODEL),
        'g_ffn2': gain(ks[17], (DEPTH, D_MODEL)),
        'w_ffn2_gu': nrm(ks[18], (DEPTH, D_MODEL, 2 * D_FF), D_MODEL),
        'w_ffn2_down': nrm(ks[19], (DEPTH, D_FF, D_MODEL), D_FF),
        'g_ple': gain(ks[20], (DEPTH, D_MODEL)),
        'w_ple_gate': nrm(ks[21], (DEPTH, D_MODEL, D_MODEL), D_MODEL),
        'w_ple': nrm(ks[22], (DEPTH, PLE_DIM, D_MODEL), PLE_DIM),
        'g_final': gain(ks[23], (D_MODEL,)),
    }


def reference(x_prompt, x_sample, p_prompt, p_sample, g_ffn1, w_ffn1_gu, w_ffn1_down, g_mix, w_in, g_q, g_k,
              g_gmlp_v, w_spatial, b_spatial, w_branch_attn, w_branch_gmlp, w_out, g_ffn2, w_ffn2_gu, w_ffn2_down,
              g_ple, w_ple_gate, w_ple, g_final):
    y_prompt = encoder_trunk(x_prompt, p_prompt, g_ffn1, w_ffn1_gu, w_ffn1_down, g_mix, w_in, g_q, g_k, g_gmlp_v,
                             w_spatial, b_spatial, w_branch_attn, w_branch_gmlp, w_out, g_ffn2, w_ffn2_gu,
                             w_ffn2_down, g_ple, w_ple_gate, w_ple, g_final)
    y_sample = encoder_trunk(x_sample, p_sample, g_ffn1, w_ffn1_gu, w_ffn1_down, g_mix, w_in, g_q, g_k, g_gmlp_v,
                             w_spatial, b_spatial, w_branch_attn, w_branch_gmlp, w_out, g_ffn2, w_ffn2_gu,
                             w_ffn2_down, g_ple, w_ple_gate, w_ple, g_final)
    return (y_prompt, y_sample)
```

```python
import functools

import jax
import jax.numpy as jnp
from jax import lax
from jax.experimental import pallas as pl
from jax.experimental.pallas import tpu as pltpu

D_MODEL = 1024
N_HEADS = 8
N_KV_HEADS = 2
HEAD_DIM = 64
Q_PER_KV = N_HEADS // N_KV_HEADS
ATTN_WIDTH = N_HEADS * HEAD_DIM
KV_WIDTH = N_KV_HEADS * HEAD_DIM
GMLP_GROUPS = 8
GMLP_GROUP_DIM = 64
GMLP_WIDTH = GMLP_GROUPS * GMLP_GROUP_DIM
CHUNK = 128
GRID_W = 64
ROPE_THETA = 10000.0
ROPE_AXIS_FREQS = HEAD_DIM // 4
D_FF = 2816
PLE_DIM = 256
EPS = 1e-6

LANES = 128
ROW_TILE = 512
Q_TILE = 256
FF_CHUNK = 256
N_FF_CHUNKS = D_FF // FF_CHUNK
V_ROWS = HEAD_DIM + 16
NEG_BIG = -1e30
VMEM_LIMIT = 56 * 1024 * 1024

_F32 = jnp.float32
_BF16 = jnp.bfloat16


def _dot(a, b):
    return jnp.dot(a, b, preferred_element_type=_F32)


def _rms(x, g):
    return x * lax.rsqrt(jnp.mean(x * x, axis=-1, keepdims=True) + EPS) * g


def _gelu(x):
    return 0.5 * x * (1.0 + lax.erf(x * (2.0 ** -0.5)))


def _group_sumsq(t, ones_blockdiag):
    sq = t * t
    hi = sq.astype(_BF16)
    lo = (sq - hi.astype(_F32)).astype(_BF16)
    return _dot(hi, ones_blockdiag) + _dot(lo, ones_blockdiag)


def _rope(t, cos, sin_signed):
    width = t.shape[-1]
    lane = lax.broadcasted_iota(jnp.int32, t.shape, 1)
    first_half = (lane & (2 * ROPE_AXIS_FREQS - 1)) < ROPE_AXIS_FREQS
    partner = jnp.where(first_half, pltpu.roll(t, width - ROPE_AXIS_FREQS, 1), pltpu.roll(t, ROPE_AXIS_FREQS, 1))
    return t * cos + partner * sin_signed


def _swiglu(xn_bf16, wgu_ref, wdn_ref, act_scr):
    for c in range(N_FF_CHUNKS):
        gu = _dot(xn_bf16, wgu_ref[:, 2 * c * FF_CHUNK:2 * (c + 1) * FF_CHUNK])
        g = gu[:, :FF_CHUNK]
        u = gu[:, FF_CHUNK:]
        act_scr[:, c * FF_CHUNK:(c + 1) * FF_CHUNK] = (g * jax.nn.sigmoid(g) * u).astype(_BF16)
    return _dot(act_scr[...], wdn_ref[...])


def _pre_kernel(x_ref, cos_ref, sin_ref, g1_ref, wgu_ref, wdn_ref, gmix_ref, win_ref, gq_ref, gk_ref, ggv_ref,
                wsp_ref, bsp_ref, wbg_ref, eq_ref, ek_ref,
                x1_ref, q_ref, k2_ref, vt_ref, sga_ref, mb_ref, act_scr):
    x = x_ref[...]
    x1 = x + 0.5 * _swiglu(_rms(x, g1_ref[...]).astype(_BF16), wgu_ref, wdn_ref, act_scr)
    x1_ref[...] = x1
    h = _rms(x1, gmix_ref[...]).astype(_BF16)

    cos = cos_ref[...]
    sin = sin_ref[...]

    q = _dot(h, win_ref[:, 0:ATTN_WIDTH])
    q = q * lax.rsqrt(_group_sumsq(q, eq_ref[...]) * (1.0 / HEAD_DIM) + EPS) * gq_ref[...]
    q = _rope(q, jnp.concatenate([cos] * (ATTN_WIDTH // LANES), axis=1),
              jnp.concatenate([sin] * (ATTN_WIDTH // LANES), axis=1))
    q_ref[...] = (q * (HEAD_DIM ** -0.5)).astype(_BF16)

    kv = _dot(h, win_ref[:, ATTN_WIDTH:ATTN_WIDTH + 2 * KV_WIDTH])
    k = kv[:, :KV_WIDTH]
    v = kv[:, KV_WIDTH:]
    k = k * lax.rsqrt(_group_sumsq(k, ek_ref[...]) * (1.0 / HEAD_DIM) + EPS) * gk_ref[...]
    k = _rope(k, cos, sin)
    k_swapped = pltpu.roll(k, HEAD_DIM, 1)
    lane = lax.broadcasted_iota(jnp.int32, k.shape, 1)
    k2_ref[0] = jnp.where(lane < HEAD_DIM, k, k_swapped).astype(_BF16)
    k2_ref[1] = jnp.where(lane < HEAD_DIM, k_swapped, k).astype(_BF16)

    vt = v.T.astype(_BF16)
    ones = jnp.ones((V_ROWS - HEAD_DIM, vt.shape[1]), _BF16)
    for kh in range(N_KV_HEADS):
        vt_ref[kh, 0, 0:HEAD_DIM, :] = vt[kh * HEAD_DIM:(kh + 1) * HEAD_DIM, :]
        vt_ref[kh, 0, HEAD_DIM:V_ROWS, :] = ones

    o_gu = ATTN_WIDTH + 2 * KV_WIDTH
    o_gv = o_gu + GMLP_WIDTH
    o_ga = o_gv + GMLP_WIDTH
    o_gb = o_ga + D_MODEL
    sga_ref[...] = jax.nn.sigmoid(_dot(h, win_ref[:, o_ga:o_gb])).astype(_BF16)

    u = _gelu(_dot(h, win_ref[:, o_gu:o_gv]))
    vg = _rms(_gelu(_dot(h, win_ref[:, o_gv:o_ga])), ggv_ref[...])
    lane_c = lax.broadcasted_iota(jnp.int32, (CHUNK, LANES), 1)
    mixed_rows = []
    for ci in range(x.shape[0] // CHUNK):
        cols = []
        for j in range(GMLP_WIDTH // LANES):
            vs = vg[ci * CHUNK:(ci + 1) * CHUNK, j * LANES:(j + 1) * LANES].astype(_BF16)
            zero = jnp.zeros_like(vs)
            rhs = jnp.concatenate([jnp.where(lane_c < GMLP_GROUP_DIM, vs, zero),
                                   jnp.where(lane_c < GMLP_GROUP_DIM, zero, vs)], axis=0)
            cols.append(_dot(wsp_ref[j], rhs))
        mixed_rows.append(jnp.concatenate(cols, axis=1) + bsp_ref[...])
    sg = (u * jnp.concatenate(mixed_rows, axis=0)).astype(_BF16)
    gate_b = jax.nn.sigmoid(_dot(h, win_ref[:, o_gb:o_gb + D_MODEL]))
    mb_ref[...] = (gate_b * _dot(sg, wbg_ref[...])).astype(_BF16)


def _attn_kernel(q_ref, k_ref, vt_ref, o_ref, qs_scr, m_scr, acc_scr, *, n_key_tiles):
    tq = q_ref.shape[0]
    lane = lax.broadcasted_iota(jnp.int32, (tq, LANES), 1)
    for hq in range(Q_PER_KV):
        pair = q_ref[:, (hq // 2) * LANES:(hq // 2 + 1) * LANES]
        keep = (lane < HEAD_DIM) if hq % 2 == 0 else (lane >= HEAD_DIM)
        qs_scr[hq * tq:(hq + 1) * tq, :] = jnp.where(keep, pair, jnp.zeros_like(pair))
    m_scr[...] = jnp.full(m_scr.shape, NEG_BIG, _F32)
    acc_scr[...] = jnp.zeros(acc_scr.shape, _F32)

    def key_tile(kt, carry):
        start = pl.multiple_of(kt * ROW_TILE, ROW_TILE)
        ks = k_ref[pl.ds(start, ROW_TILE), :]
        s = lax.dot_general(ks, qs_scr[...], (((1,), (1,)), ((), ())), preferred_element_type=_F32)
        m_old = m_scr[...]
        m_new = jnp.maximum(m_old, jnp.max(s, axis=0, keepdims=True))
        p = jnp.exp(s - m_new).astype(_BF16)
        acc_scr[...] = acc_scr[...] * jnp.exp(m_old - m_new) + _dot(vt_ref[kt], p)
        m_scr[...] = m_new
        return carry

    lax.fori_loop(0, n_key_tiles, key_tile, 0)

    acc = acc_scr[...]
    out_t = acc[0:HEAD_DIM, :] / acc[HEAD_DIM:HEAD_DIM + 1, :]
    for j in range(Q_PER_KV // 2):
        pair_t = jnp.concatenate([out_t[:, (2 * j) * tq:(2 * j + 1) * tq],
                                  out_t[:, (2 * j + 1) * tq:(2 * j + 2) * tq]], axis=0)
        o_ref[:, j * LANES:(j + 1) * LANES] = pair_t.T.astype(_BF16)


def _post_kernel(a_ref, sga_ref, mb_ref, x1_ref, p_ref, wba_ref, wout_ref, g2_ref, wgu_ref, wdn_ref, gple_ref,
                 wpg_ref, wple_ref, gfin_ref, y_ref, act_scr):
    merged = sga_ref[...].astype(_F32) * _dot(a_ref[...], wba_ref[...]) + mb_ref[...].astype(_F32)
    x2 = x1_ref[...] + _dot(merged.astype(_BF16), wout_ref[...])
    x3 = x2 + 0.5 * _swiglu(_rms(x2, g2_ref[...]).astype(_BF16), wgu_ref, wdn_ref, act_scr)
    gate = jax.nn.sigmoid(_dot(_rms(x3, gple_ref[...]).astype(_BF16), wpg_ref[...]))
    x4 = x3 + gate * _dot(p_ref[...].astype(_BF16), wple_ref[...])
    y_ref[...] = _rms(x4, gfin_ref[...])


def _resident(shape):
    nd = len(shape)
    return pl.BlockSpec(shape, lambda *_: (0,) * nd, pipeline_mode=pl.Buffered(1))


def _rows(width, tile=ROW_TILE):
    return pl.BlockSpec((tile, width), lambda i: (i, 0))


def _interleave_gu(w_gu):
    d = w_gu.shape[0]
    g = w_gu[:, :D_FF].reshape(d, N_FF_CHUNKS, FF_CHUNK)
    u = w_gu[:, D_FF:].reshape(d, N_FF_CHUNKS, FF_CHUNK)
    return jnp.stack([g, u], axis=2).reshape(d, 2 * D_FF).astype(_BF16)


def _rope_tables(seq):
    rows = seq // GRID_W
    row = jnp.repeat(jnp.arange(rows, dtype=_F32), GRID_W)
    col = jnp.tile(jnp.arange(GRID_W, dtype=_F32), rows)
    inv = jnp.power(jnp.float32(ROPE_THETA), -jnp.arange(ROPE_AXIS_FREQS, dtype=_F32) / ROPE_AXIS_FREQS)
    ang = jnp.stack([row[:, None] * inv, col[:, None] * inv], axis=1)
    cos = jnp.cos(ang)
    sin = jnp.sin(ang)
    cos_h = jnp.concatenate([cos[:, 0], cos[:, 0], cos[:, 1], cos[:, 1]], axis=1)
    sin_h = jnp.concatenate([-sin[:, 0], sin[:, 0], -sin[:, 1], sin[:, 1]], axis=1)
    return jnp.tile(cos_h, (1, LANES // HEAD_DIM)), jnp.tile(sin_h, (1, LANES // HEAD_DIM))


def _block_diag_ones(width):
    idx = jnp.arange(width) // HEAD_DIM
    return (idx[:, None] == idx[None, :]).astype(_BF16)


def _trunk(x, p, prm, cos_t, sin_t):
    batch, seq, _ = x.shape
    rows = batch * seq
    n_tiles = rows // ROW_TILE
    seq_tiles = seq // ROW_TILE
    x2d = x.reshape(rows, D_MODEL)
    p2d = p.reshape(rows, PLE_DIM)
    cparams = pltpu.CompilerParams(dimension_semantics=("arbitrary",), vmem_limit_bytes=VMEM_LIMIT)

    pos_spec = pl.BlockSpec((ROW_TILE, LANES), lambda i: (i % seq_tiles, 0))
    x1, q, k2, vt, sga, mb = pl.pallas_call(
        _pre_kernel,
        grid=(n_tiles,),
        in_specs=[_rows(D_MODEL), pos_spec, pos_spec,
                  _resident((1, D_MODEL)), _resident((D_MODEL, 2 * D_FF)), _resident((D_FF, D_MODEL)),
                  _resident((1, D_MODEL)), _resident(prm["w_in"].shape),
                  _resident((1, ATTN_WIDTH)), _resident((1, KV_WIDTH)), _resident((1, GMLP_WIDTH)),
                  _resident(prm["w_sp"].shape), _resident((CHUNK, GMLP_WIDTH)), _resident((GMLP_WIDTH, D_MODEL)),
                  _resident((ATTN_WIDTH, ATTN_WIDTH)), _resident((KV_WIDTH, KV_WIDTH))],
        out_specs=[_rows(D_MODEL), _rows(ATTN_WIDTH),
                   pl.BlockSpec((N_KV_HEADS, ROW_TILE, LANES), lambda i: (0, i, 0)),
                   pl.BlockSpec((N_KV_HEADS, 1, V_ROWS, ROW_TILE), lambda i: (0, i, 0, 0)),
                   _rows(D_MODEL), _rows(D_MODEL)],
        out_shape=[jax.ShapeDtypeStruct((rows, D_MODEL), _F32),
                   jax.ShapeDtypeStruct((rows, ATTN_WIDTH), _BF16),
                   jax.ShapeDtypeStruct((N_KV_HEADS, rows, LANES), _BF16),
                   jax.ShapeDtypeStruct((N_KV_HEADS, n_tiles, V_ROWS, ROW_TILE), _BF16),
                   jax.ShapeDtypeStruct((rows, D_MODEL), _BF16),
                   jax.ShapeDtypeStruct((rows, D_MODEL), _BF16)],
        scratch_shapes=[pltpu.VMEM((ROW_TILE, D_FF), _BF16)],
        compiler_params=cparams,
        name="pre",
    )(x2d, cos_t, sin_t, prm["g_ffn1"], prm["w_gu1"], prm["w_dn1"], prm["g_mix"], prm["w_in"],
      prm["g_q"], prm["g_k"], prm["g_gv"], prm["w_sp"], prm["b_sp"], prm["w_bg"], prm["e_q"], prm["e_k"])

    q_tiles = seq // Q_TILE
    a = pl.pallas_call(
        functools.partial(_attn_kernel, n_key_tiles=seq_tiles),
        grid=(batch, N_KV_HEADS, q_tiles),
        in_specs=[pl.BlockSpec((Q_TILE, 2 * LANES), lambda b, kh, qi: (b * q_tiles + qi, kh)),
                  pl.BlockSpec((None, seq, LANES), lambda b, kh, qi: (kh, b, 0)),
                  pl.BlockSpec((None, seq_tiles, V_ROWS, ROW_TILE), lambda b, kh, qi: (kh, b, 0, 0))],
        out_specs=pl.BlockSpec((Q_TILE, 2 * LANES), lambda b, kh, qi: (b * q_tiles + qi, kh)),
        out_shape=jax.ShapeDtypeStruct((rows, ATTN_WIDTH), _BF16),
        scratch_shapes=[pltpu.VMEM((Q_PER_KV * Q_TILE, LANES), _BF16),
                        pltpu.VMEM((1, Q_PER_KV * Q_TILE), _F32),
                        pltpu.VMEM((V_ROWS, Q_PER_KV * Q_TILE), _F32)],
        compiler_params=pltpu.CompilerParams(dimension_semantics=("arbitrary",) * 3, vmem_limit_bytes=VMEM_LIMIT),
        name="attn",
    )(q, k2, vt)

    y = pl.pallas_call(
        _post_kernel,
        grid=(n_tiles,),
        in_specs=[_rows(ATTN_WIDTH), _rows(D_MODEL), _rows(D_MODEL), _rows(D_MODEL), _rows(PLE_DIM),
                  _resident((ATTN_WIDTH, D_MODEL)), _resident((D_MODEL, D_MODEL)),
                  _resident((1, D_MODEL)), _resident((D_MODEL, 2 * D_FF)), _resident((D_FF, D_MODEL)),
                  _resident((1, D_MODEL)), _resident((D_MODEL, D_MODEL)), _resident((PLE_DIM, D_MODEL)),
                  _resident((1, D_MODEL))],
        out_specs=_rows(D_MODEL),
        out_shape=jax.ShapeDtypeStruct((rows, D_MODEL), _F32),
        scratch_shapes=[pltpu.VMEM((ROW_TILE, D_FF), _BF16)],
        compiler_params=cparams,
        name="post",
    )(a, sga, mb, x1, p2d, prm["w_ba"], prm["w_out"], prm["g_ffn2"], prm["w_gu2"], prm["w_dn2"],
      prm["g_ple"], prm["w_pg"], prm["w_ple"], prm["g_final"])
    return y.reshape(batch, seq, D_MODEL)


def kernel(x_prompt, x_sample, p_prompt, p_sample, g_ffn1, w_ffn1_gu, w_ffn1_down, g_mix, w_in, g_q, g_k, g_gmlp_v, w_spatial, b_spatial, w_branch_attn, w_branch_gmlp, w_out, g_ffn2, w_ffn2_gu, w_ffn2_down, g_ple, w_ple_gate, w_ple, g_final):
    assert g_ffn1.shape[0] == 1, "the post kernel fuses the final norm into the single layer"
    cos_t, sin_t = _rope_tables(max(x_prompt.shape[1], x_sample.shape[1]))
    w_sp = w_spatial[0].reshape(GMLP_GROUPS // 2, 2, CHUNK, CHUNK).transpose(0, 2, 1, 3)
    prm = dict(
        g_ffn1=g_ffn1[0][None], w_gu1=_interleave_gu(w_ffn1_gu[0]), w_dn1=w_ffn1_down[0].astype(_BF16),
        g_mix=g_mix[0][None], w_in=w_in[0].astype(_BF16),
        g_q=jnp.tile(g_q[0], N_HEADS)[None], g_k=jnp.tile(g_k[0], N_KV_HEADS)[None],
        g_gv=g_gmlp_v[0][None],
        w_sp=w_sp.reshape(GMLP_GROUPS // 2, CHUNK, 2 * CHUNK).astype(_BF16),
        b_sp=jnp.repeat(b_spatial[0].T, GMLP_GROUP_DIM, axis=1),
        w_bg=w_branch_gmlp[0].astype(_BF16), w_ba=w_branch_attn[0].astype(_BF16),
        w_out=w_out[0].astype(_BF16), g_ffn2=g_ffn2[0][None], w_gu2=_interleave_gu(w_ffn2_gu[0]),
        w_dn2=w_ffn2_down[0].astype(_BF16), g_ple=g_ple[0][None], w_pg=w_ple_gate[0].astype(_BF16),
        w_ple=w_ple[0].astype(_BF16), g_final=g_final[None],
        e_q=_block_diag_ones(ATTN_WIDTH), e_k=_block_diag_ones(KV_WIDTH),
    )
    return (_trunk(x_prompt, p_prompt[0], prm, cos_t, sin_t), _trunk(x_sample, p_sample[0], prm, cos_t, sin_t))
```

```python
import functools

import jax
import jax.numpy as jnp
from jax import lax
from jax.experimental import pallas as pl
from jax.experimental.pallas import tpu as pltpu

D_MODEL = 1024
N_HEADS = 8
N_KV_HEADS = 2
HEAD_DIM = 64
Q_PER_KV = N_HEADS // N_KV_HEADS
ATTN_WIDTH = N_HEADS * HEAD_DIM
KV_WIDTH = N_KV_HEADS * HEAD_DIM
GMLP_GROUPS = 8
GMLP_GROUP_DIM = 64
GMLP_WIDTH = GMLP_GROUPS * GMLP_GROUP_DIM
CHUNK = 128
GRID_W = 64
ROPE_THETA = 10000.0
ROPE_AXIS_FREQS = HEAD_DIM // 4
D_FF = 2816
PLE_DIM = 256
EPS = 1e-6

LANES = 128
ROW_TILE = 512
Q_TILE = 256
KEY_TILES_PER_ITER = 4
FF_CHUNK = 256
N_FF_CHUNKS = D_FF // FF_CHUNK
V_ROWS = HEAD_DIM + 16
NEG_BIG = -1e30
LOG2_E = 1.4426950408889634
VMEM_LIMIT = 56 * 1024 * 1024

_F32 = jnp.float32
_BF16 = jnp.bfloat16


def _dot(a, b):
    return jnp.dot(a, b, preferred_element_type=_F32)


def _rms(x, g):
    return x * lax.rsqrt(jnp.mean(x * x, axis=-1, keepdims=True) + EPS) * g


def _gelu(x):
    return 0.5 * x * (1.0 + lax.erf(x * (2.0 ** -0.5)))


def _group_sumsq(t, ones_blockdiag):
    sq = t * t
    hi = sq.astype(_BF16)
    lo = (sq - hi.astype(_F32)).astype(_BF16)
    return _dot(hi, ones_blockdiag) + _dot(lo, ones_blockdiag)


def _rope(t, cos, sin_signed):
    width = t.shape[-1]
    lane = lax.broadcasted_iota(jnp.int32, t.shape, 1)
    first_half = (lane & (2 * ROPE_AXIS_FREQS - 1)) < ROPE_AXIS_FREQS
    partner = jnp.where(first_half, pltpu.roll(t, width - ROPE_AXIS_FREQS, 1), pltpu.roll(t, ROPE_AXIS_FREQS, 1))
    return t * cos + partner * sin_signed


def _swiglu(xn_bf16, wgu_ref, wdn_ref, act_scr):
    for c in range(N_FF_CHUNKS):
        gu = _dot(xn_bf16, wgu_ref[:, 2 * c * FF_CHUNK:2 * (c + 1) * FF_CHUNK])
        g = gu[:, :FF_CHUNK]
        u = gu[:, FF_CHUNK:]
        act_scr[:, c * FF_CHUNK:(c + 1) * FF_CHUNK] = (g * jax.nn.sigmoid(g) * u).astype(_BF16)
    return _dot(act_scr[...], wdn_ref[...])


def _pre_kernel(x_ref, cos_ref, sin_ref, g1_ref, wgu_ref, wdn_ref, gmix_ref, win_ref, gq_ref, gk_ref, ggv_ref,
                wsp_ref, bsp_ref, wbg_ref, eq_ref, ek_ref,
                x1_ref, q_ref, k2_ref, vt_ref, sga_ref, mb_ref, act_scr):
    x = x_ref[...]
    x1 = x + 0.5 * _swiglu(_rms(x, g1_ref[...]).astype(_BF16), wgu_ref, wdn_ref, act_scr)
    x1_ref[...] = x1
    h = _rms(x1, gmix_ref[...]).astype(_BF16)

    cos = cos_ref[...]
    sin = sin_ref[...]

    q = _dot(h, win_ref[:, 0:ATTN_WIDTH])
    q = q * lax.rsqrt(_group_sumsq(q, eq_ref[...]) * (1.0 / HEAD_DIM) + EPS) * gq_ref[...]
    q = _rope(q, jnp.concatenate([cos] * (ATTN_WIDTH // LANES), axis=1),
              jnp.concatenate([sin] * (ATTN_WIDTH // LANES), axis=1))
    q_ref[...] = (q * (HEAD_DIM ** -0.5 * LOG2_E)).astype(_BF16)

    kv = _dot(h, win_ref[:, ATTN_WIDTH:ATTN_WIDTH + 2 * KV_WIDTH])
    k = kv[:, :KV_WIDTH]
    v = kv[:, KV_WIDTH:]
    k = k * lax.rsqrt(_group_sumsq(k, ek_ref[...]) * (1.0 / HEAD_DIM) + EPS) * gk_ref[...]
    k = _rope(k, cos, sin)
    k_swapped = pltpu.roll(k, HEAD_DIM, 1)
    lane = lax.broadcasted_iota(jnp.int32, k.shape, 1)
    k2_ref[0] = jnp.where(lane < HEAD_DIM, k, k_swapped).astype(_BF16)
    k2_ref[1] = jnp.where(lane < HEAD_DIM, k_swapped, k).astype(_BF16)

    vt = v.T.astype(_BF16)
    ones = jnp.ones((V_ROWS - HEAD_DIM, vt.shape[1]), _BF16)
    for kh in range(N_KV_HEADS):
        vt_ref[kh, 0, 0:HEAD_DIM, :] = vt[kh * HEAD_DIM:(kh + 1) * HEAD_DIM, :]
        vt_ref[kh, 0, HEAD_DIM:V_ROWS, :] = ones

    o_gu = ATTN_WIDTH + 2 * KV_WIDTH
    o_gv = o_gu + GMLP_WIDTH
    o_ga = o_gv + GMLP_WIDTH
    o_gb = o_ga + D_MODEL
    sga_ref[...] = jax.nn.sigmoid(_dot(h, win_ref[:, o_ga:o_gb])).astype(_BF16)

    u = _gelu(_dot(h, win_ref[:, o_gu:o_gv]))
    vg = _rms(_gelu(_dot(h, win_ref[:, o_gv:o_ga])), ggv_ref[...])
    lane_c = lax.broadcasted_iota(jnp.int32, (CHUNK, LANES), 1)
    mixed_rows = []
    for ci in range(x.shape[0] // CHUNK):
        cols = []
        for j in range(GMLP_WIDTH // LANES):
            vs = vg[ci * CHUNK:(ci + 1) * CHUNK, j * LANES:(j + 1) * LANES].astype(_BF16)
            zero = jnp.zeros_like(vs)
            rhs = jnp.concatenate([jnp.where(lane_c < GMLP_GROUP_DIM, vs, zero),
                                   jnp.where(lane_c < GMLP_GROUP_DIM, zero, vs)], axis=0)
            cols.append(_dot(wsp_ref[j], rhs))
        mixed_rows.append(jnp.concatenate(cols, axis=1) + bsp_ref[...])
    sg = (u * jnp.concatenate(mixed_rows, axis=0)).astype(_BF16)
    gate_b = jax.nn.sigmoid(_dot(h, win_ref[:, o_gb:o_gb + D_MODEL]))
    mb_ref[...] = (gate_b * _dot(sg, wbg_ref[...])).astype(_BF16)


def _attn_kernel(q_ref, k_ref, vt_ref, o_ref, qs_scr, m_scr, acc_scr, s_scr, t_scr, *, n_key_tiles):
    tq = q_ref.shape[0]
    lane = lax.broadcasted_iota(jnp.int32, (tq, LANES), 1)
    for hq in range(Q_PER_KV):
        pair = q_ref[:, (hq // 2) * LANES:(hq // 2 + 1) * LANES]
        keep = (lane < HEAD_DIM) if hq % 2 == 0 else (lane >= HEAD_DIM)
        qs_scr[hq * tq:(hq + 1) * tq, :] = jnp.where(keep, pair, jnp.zeros_like(pair))
    m_scr[...] = jnp.full(m_scr.shape, NEG_BIG, _F32)
    acc_scr[...] = jnp.zeros(acc_scr.shape, _F32)

    def scores(kt, slot):
        start = pl.multiple_of(kt * ROW_TILE, ROW_TILE)
        ks = k_ref[pl.ds(start, ROW_TILE), :]
        for c in range(Q_PER_KV):
            cols = slice(c * tq, (c + 1) * tq)
            s = lax.dot_general(ks, qs_scr[cols, :], (((1,), (1,)), ((), ())), preferred_element_type=_F32)
            s_scr[slot, :, cols] = s
            t_scr[slot, :, cols] = jnp.max(s, axis=0, keepdims=True)

    scores(0, 0)

    def consume(kt, slot):
        for c in range(Q_PER_KV):
            cols = slice(c * tq, (c + 1) * tq)
            m_old = m_scr[:, cols]
            m_new = jnp.maximum(m_old, t_scr[slot, :, cols])
            p = jnp.exp2(s_scr[slot, :, cols] - m_new).astype(_BF16)
            acc_scr[:, cols] = acc_scr[:, cols] * jnp.exp2(m_old - m_new) + _dot(vt_ref[kt], p)
            m_scr[:, cols] = m_new

    def key_tile_group(i, carry):
        for j in range(KEY_TILES_PER_ITER):
            kt = KEY_TILES_PER_ITER * i + j
            scores(kt + 1 if j + 1 < KEY_TILES_PER_ITER else jnp.minimum(kt + 1, n_key_tiles - 1), (j + 1) % 2)
            consume(kt, j % 2)
        return carry

    assert n_key_tiles % KEY_TILES_PER_ITER == 0 and KEY_TILES_PER_ITER % 2 == 0
    lax.fori_loop(0, n_key_tiles // KEY_TILES_PER_ITER, key_tile_group, 0)

    acc = acc_scr[...]
    out_t = acc[0:HEAD_DIM, :] / acc[HEAD_DIM:HEAD_DIM + 1, :]
    for j in range(Q_PER_KV // 2):
        pair_t = jnp.concatenate([out_t[:, (2 * j) * tq:(2 * j + 1) * tq],
                                  out_t[:, (2 * j + 1) * tq:(2 * j + 2) * tq]], axis=0)
        o_ref[:, j * LANES:(j + 1) * LANES] = pair_t.T.astype(_BF16)


def _post_kernel(a_ref, sga_ref, mb_ref, x1_ref, p_ref, wba_ref, wout_ref, g2_ref, wgu_ref, wdn_ref, gple_ref,
                 wpg_ref, wple_ref, gfin_ref, y_ref, act_scr):
    merged = sga_ref[...].astype(_F32) * _dot(a_ref[...], wba_ref[...]) + mb_ref[...].astype(_F32)
    x2 = x1_ref[...] + _dot(merged.astype(_BF16), wout_ref[...])
    x3 = x2 + 0.5 * _swiglu(_rms(x2, g2_ref[...]).astype(_BF16), wgu_ref, wdn_ref, act_scr)
    gate = jax.nn.sigmoid(_dot(_rms(x3, gple_ref[...]).astype(_BF16), wpg_ref[...]))
    x4 = x3 + gate * _dot(p_ref[...].astype(_BF16), wple_ref[...])
    y_ref[...] = _rms(x4, gfin_ref[...])


def _resident(shape):
    nd = len(shape)
    return pl.BlockSpec(shape, lambda *_: (0,) * nd, pipeline_mode=pl.Buffered(1))


def _rows(width, tile=ROW_TILE):
    return pl.BlockSpec((tile, width), lambda i: (i, 0))


def _interleave_gu(w_gu):
    d = w_gu.shape[0]
    g = w_gu[:, :D_FF].reshape(d, N_FF_CHUNKS, FF_CHUNK)
    u = w_gu[:, D_FF:].reshape(d, N_FF_CHUNKS, FF_CHUNK)
    return jnp.stack([g, u], axis=2).reshape(d, 2 * D_FF).astype(_BF16)


def _rope_tables(seq):
    rows = seq // GRID_W
    row = jnp.repeat(jnp.arange(rows, dtype=_F32), GRID_W)
    col = jnp.tile(jnp.arange(GRID_W, dtype=_F32), rows)
    inv = jnp.power(jnp.float32(ROPE_THETA), -jnp.arange(ROPE_AXIS_FREQS, dtype=_F32) / ROPE_AXIS_FREQS)
    ang = jnp.stack([row[:, None] * inv, col[:, None] * inv], axis=1)
    cos = jnp.cos(ang)
    sin = jnp.sin(ang)
    cos_h = jnp.concatenate([cos[:, 0], cos[:, 0], cos[:, 1], cos[:, 1]], axis=1)
    sin_h = jnp.concatenate([-sin[:, 0], sin[:, 0], -sin[:, 1], sin[:, 1]], axis=1)
    return jnp.tile(cos_h, (1, LANES // HEAD_DIM)), jnp.tile(sin_h, (1, LANES // HEAD_DIM))


def _block_diag_ones(width):
    idx = jnp.arange(width) // HEAD_DIM
    return (idx[:, None] == idx[None, :]).astype(_BF16)


def _trunk(x, p, prm, cos_t, sin_t):
    batch, seq, _ = x.shape
    rows = batch * seq
    n_tiles = rows // ROW_TILE
    seq_tiles = seq // ROW_TILE
    x2d = x.reshape(rows, D_MODEL)
    p2d = p.reshape(rows, PLE_DIM)
    cparams = pltpu.CompilerParams(dimension_semantics=("arbitrary",), vmem_limit_bytes=VMEM_LIMIT)

    pos_spec = pl.BlockSpec((ROW_TILE, LANES), lambda i: (i % seq_tiles, 0))
    x1, q, k2, vt, sga, mb = pl.pallas_call(
        _pre_kernel,
        grid=(n_tiles,),
        in_specs=[_rows(D_MODEL), pos_spec, pos_spec,
                  _resident((1, D_MODEL)), _resident((D_MODEL, 2 * D_FF)), _resident((D_FF, D_MODEL)),
                  _resident((1, D_MODEL)), _resident(prm["w_in"].shape),
                  _resident((1, ATTN_WIDTH)), _resident((1, KV_WIDTH)), _resident((1, GMLP_WIDTH)),
                  _resident(prm["w_sp"].shape), _resident((CHUNK, GMLP_WIDTH)), _resident((GMLP_WIDTH, D_MODEL)),
                  _resident((ATTN_WIDTH, ATTN_WIDTH)), _resident((KV_WIDTH, KV_WIDTH))],
        out_specs=[_rows(D_MODEL), _rows(ATTN_WIDTH),
                   pl.BlockSpec((N_KV_HEADS, ROW_TILE, LANES), lambda i: (0, i, 0)),
                   pl.BlockSpec((N_KV_HEADS, 1, V_ROWS, ROW_TILE), lambda i: (0, i, 0, 0)),
                   _rows(D_MODEL), _rows(D_MODEL)],
        out_shape=[jax.ShapeDtypeStruct((rows, D_MODEL), _F32),
                   jax.ShapeDtypeStruct((rows, ATTN_WIDTH), _BF16),
                   jax.ShapeDtypeStruct((N_KV_HEADS, rows, LANES), _BF16),
                   jax.ShapeDtypeStruct((N_KV_HEADS, n_tiles, V_ROWS, ROW_TILE), _BF16),
                   jax.ShapeDtypeStruct((rows, D_MODEL), _BF16),
                   jax.ShapeDtypeStruct((rows, D_MODEL), _BF16)],
        scratch_shapes=[pltpu.VMEM((ROW_TILE, D_FF), _BF16)],
        compiler_params=cparams,
        name="pre",
    )(x2d, cos_t, sin_t, prm["g_ffn1"], prm["w_gu1"], prm["w_dn1"], prm["g_mix"], prm["w_in"],
      prm["g_q"], prm["g_k"], prm["g_gv"], prm["w_sp"], prm["b_sp"], prm["w_bg"], prm["e_q"], prm["e_k"])

    q_tiles = seq // Q_TILE
    a = pl.pallas_call(
        functools.partial(_attn_kernel, n_key_tiles=seq_tiles),
        grid=(batch, N_KV_HEADS, q_tiles),
        in_specs=[pl.BlockSpec((Q_TILE, 2 * LANES), lambda b, kh, qi: (b * q_tiles + qi, kh)),
                  pl.BlockSpec((None, seq, LANES), lambda b, kh, qi: (kh, b, 0)),
                  pl.BlockSpec((None, seq_tiles, V_ROWS, ROW_TILE), lambda b, kh, qi: (kh, b, 0, 0))],
        out_specs=pl.BlockSpec((Q_TILE, 2 * LANES), lambda b, kh, qi: (b * q_tiles + qi, kh)),
        out_shape=jax.ShapeDtypeStruct((rows, ATTN_WIDTH), _BF16),
        scratch_shapes=[pltpu.VMEM((Q_PER_KV * Q_TILE, LANES), _BF16),
                        pltpu.VMEM((1, Q_PER_KV * Q_TILE), _F32),
                        pltpu.VMEM((V_ROWS, Q_PER_KV * Q_TILE), _F32),
                        pltpu.VMEM((2, ROW_TILE, Q_PER_KV * Q_TILE), _F32),
                        pltpu.VMEM((2, 1, Q_PER_KV * Q_TILE), _F32)],
        compiler_params=pltpu.CompilerParams(dimension_semantics=("arbitrary",) * 3, vmem_limit_bytes=VMEM_LIMIT),
        name="attn",
    )(q, k2, vt)

    y = pl.pallas_call(
        _post_kernel,
        grid=(n_tiles,),
        in_specs=[_rows(ATTN_WIDTH), _rows(D_MODEL), _rows(D_MODEL), _rows(D_MODEL), _rows(PLE_DIM),
                  _resident((ATTN_WIDTH, D_MODEL)), _resident((D_MODEL, D_MODEL)),
                  _resident((1, D_MODEL)), _resident((D_MODEL, 2 * D_FF)), _resident((D_FF, D_MODEL)),
                  _resident((1, D_MODEL)), _resident((D_MODEL, D_MODEL)), _resident((PLE_DIM, D_MODEL)),
                  _resident((1, D_MODEL))],
        out_specs=_rows(D_MODEL),
        out_shape=jax.ShapeDtypeStruct((rows, D_MODEL), _F32),
        scratch_shapes=[pltpu.VMEM((ROW_TILE, D_FF), _BF16)],
        compiler_params=cparams,
        name="post",
    )(a, sga, mb, x1, p2d, prm["w_ba"], prm["w_out"], prm["g_ffn2"], prm["w_gu2"], prm["w_dn2"],
      prm["g_ple"], prm["w_pg"], prm["w_ple"], prm["g_final"])
    return y.reshape(batch, seq, D_MODEL)


def kernel(x_prompt, x_sample, p_prompt, p_sample, g_ffn1, w_ffn1_gu, w_ffn1_down, g_mix, w_in, g_q, g_k, g_gmlp_v, w_spatial, b_spatial, w_branch_attn, w_branch_gmlp, w_out, g_ffn2, w_ffn2_gu, w_ffn2_down, g_ple, w_ple_gate, w_ple, g_final):
    assert g_ffn1.shape[0] == 1, "the post kernel fuses the final norm into the single layer"
    cos_t, sin_t = _rope_tables(max(x_prompt.shape[1], x_sample.shape[1]))
    w_sp = w_spatial[0].reshape(GMLP_GROUPS // 2, 2, CHUNK, CHUNK).transpose(0, 2, 1, 3)
    prm = dict(
        g_ffn1=g_ffn1[0][None], w_gu1=_interleave_gu(w_ffn1_gu[0]), w_dn1=w_ffn1_down[0].astype(_BF16),
        g_mix=g_mix[0][None], w_in=w_in[0].astype(_BF16),
        g_q=jnp.tile(g_q[0], N_HEADS)[None], g_k=jnp.tile(g_k[0], N_KV_HEADS)[None],
        g_gv=g_gmlp_v[0][None],
        w_sp=w_sp.reshape(GMLP_GROUPS // 2, CHUNK, 2 * CHUNK).astype(_BF16),
        b_sp=jnp.repeat(b_spatial[0].T, GMLP_GROUP_DIM, axis=1),
        w_bg=w_branch_gmlp[0].astype(_BF16), w_ba=w_branch_attn[0].astype(_BF16),
        w_out=w_out[0].astype(_BF16), g_ffn2=g_ffn2[0][None], w_gu2=_interleave_gu(w_ffn2_gu[0]),
        w_dn2=w_ffn2_down[0].astype(_BF16), g_ple=g_ple[0][None], w_pg=w_ple_gate[0].astype(_BF16),
        w_ple=w_ple[0].astype(_BF16), g_final=g_final[None],
        e_q=_block_diag_ones(ATTN_WIDTH), e_k=_block_diag_ones(KV_WIDTH),
    )
    return (_trunk(x_prompt, p_prompt[0], prm, cos_t, sin_t), _trunk(x_sample, p_sample[0], prm, cos_t, sin_t))
```

```python
import functools

import jax
import jax.numpy as jnp
from jax import lax
from jax.experimental import pallas as pl
from jax.experimental.pallas import tpu as pltpu

D_MODEL = 1024
N_HEADS = 8
N_KV_HEADS = 2
HEAD_DIM = 64
Q_PER_KV = N_HEADS // N_KV_HEADS
ATTN_WIDTH = N_HEADS * HEAD_DIM
KV_WIDTH = N_KV_HEADS * HEAD_DIM
GMLP_GROUPS = 8
GMLP_GROUP_DIM = 64
GMLP_WIDTH = GMLP_GROUPS * GMLP_GROUP_DIM
CHUNK = 128
GRID_W = 64
ROPE_THETA = 10000.0
ROPE_AXIS_FREQS = HEAD_DIM // 4
D_FF = 2816
PLE_DIM = 256
EPS = 1e-6

LANES = 128
ROW_TILE = 512
KEY_TILE = 512
Q_TILE = 256
KEY_TILES_PER_ITER = 4
FF_CHUNK = 256
N_FF_CHUNKS = D_FF // FF_CHUNK
V_ROWS = HEAD_DIM + 16
NEG_BIG = -1e30
LOG2_E = 1.4426950408889634
SAFE_SHIFT_MAX = 50.0
SHIFT_MARGIN = 1.01
VMEM_LIMIT = 56 * 1024 * 1024

_F32 = jnp.float32
_BF16 = jnp.bfloat16


def _dot(a, b):
    return jnp.dot(a, b, preferred_element_type=_F32)


def _rms(x, g):
    return x * lax.rsqrt(jnp.mean(x * x, axis=-1, keepdims=True) + EPS) * g


def _gelu(x):
    return 0.5 * x * (1.0 + lax.erf(x * (2.0 ** -0.5)))


def _group_sumsq(t, ones_blockdiag):
    sq = t * t
    hi = sq.astype(_BF16)
    lo = (sq - hi.astype(_F32)).astype(_BF16)
    return _dot(hi, ones_blockdiag) + _dot(lo, ones_blockdiag)


def _rope(t, cos, sin_signed):
    width = t.shape[-1]
    lane = lax.broadcasted_iota(jnp.int32, t.shape, 1)
    first_half = (lane & (2 * ROPE_AXIS_FREQS - 1)) < ROPE_AXIS_FREQS
    partner = jnp.where(first_half, pltpu.roll(t, width - ROPE_AXIS_FREQS, 1), pltpu.roll(t, ROPE_AXIS_FREQS, 1))
    return t * cos + partner * sin_signed


def _swiglu(xn_bf16, wgu_ref, wdn_ref, act_scr):
    for c in range(N_FF_CHUNKS):
        g = _dot(xn_bf16, wgu_ref[:, c * FF_CHUNK:(c + 1) * FF_CHUNK])
        u = _dot(xn_bf16, wgu_ref[:, D_FF + c * FF_CHUNK:D_FF + (c + 1) * FF_CHUNK])
        act_scr[:, c * FF_CHUNK:(c + 1) * FF_CHUNK] = (g * jax.nn.sigmoid(g) * u).astype(_BF16)
    return _dot(act_scr[...], wdn_ref[...])


def _pre_kernel(x_ref, cos_ref, sin_ref, g1_ref, wgu_ref, wdn_ref, gmix_ref, win_ref, gq_ref, gk_ref, ggv_ref,
                wsp_ref, bsp_ref, wbg_ref, eq_ref, ek_ref,
                x1_ref, q_ref, k2_ref, vt_ref, kn_ref, sga_ref, mb_ref, act_scr):
    x = x_ref[...]
    x1 = x + 0.5 * _swiglu(_rms(x, g1_ref[...]).astype(_BF16), wgu_ref, wdn_ref, act_scr)
    x1_ref[...] = x1
    h = _rms(x1, gmix_ref[...]).astype(_BF16)

    cos = cos_ref[...]
    sin = sin_ref[...]

    q = _dot(h, win_ref[:, 0:ATTN_WIDTH])
    q = q * lax.rsqrt(_group_sumsq(q, eq_ref[...]) * (1.0 / HEAD_DIM) + EPS) * gq_ref[...]
    q = _rope(q, jnp.concatenate([cos] * (ATTN_WIDTH // LANES), axis=1),
              jnp.concatenate([sin] * (ATTN_WIDTH // LANES), axis=1))
    q_ref[...] = (q * (HEAD_DIM ** -0.5 * LOG2_E)).astype(_BF16)

    kv = _dot(h, win_ref[:, ATTN_WIDTH:ATTN_WIDTH + 2 * KV_WIDTH])
    k = kv[:, :KV_WIDTH]
    v = kv[:, KV_WIDTH:]
    k = k * lax.rsqrt(_group_sumsq(k, ek_ref[...]) * (1.0 / HEAD_DIM) + EPS) * gk_ref[...]
    k = _rope(k, cos, sin)
    kn_ref[0] = jnp.broadcast_to(jnp.max(_group_sumsq(k, ek_ref[...]), axis=0, keepdims=True), kn_ref.shape[1:])
    k_swapped = pltpu.roll(k, HEAD_DIM, 1)
    lane = lax.broadcasted_iota(jnp.int32, k.shape, 1)
    k2_ref[0] = jnp.where(lane < HEAD_DIM, k, k_swapped).astype(_BF16)
    k2_ref[1] = jnp.where(lane < HEAD_DIM, k_swapped, k).astype(_BF16)

    vt = v.T.astype(_BF16)
    ones = jnp.ones((V_ROWS - HEAD_DIM, KEY_TILE), _BF16)
    for kh in range(N_KV_HEADS):
        for t in range(vt.shape[1] // KEY_TILE):
            vt_ref[kh, t, 0:HEAD_DIM, :] = vt[kh * HEAD_DIM:(kh + 1) * HEAD_DIM, t * KEY_TILE:(t + 1) * KEY_TILE]
            vt_ref[kh, t, HEAD_DIM:V_ROWS, :] = ones

    o_gu = ATTN_WIDTH + 2 * KV_WIDTH
    o_gv = o_gu + GMLP_WIDTH
    o_ga = o_gv + GMLP_WIDTH
    o_gb = o_ga + D_MODEL
    sga_ref[...] = jax.nn.sigmoid(_dot(h, win_ref[:, o_ga:o_gb])).astype(_BF16)

    u = _gelu(_dot(h, win_ref[:, o_gu:o_gv]))
    vg = _rms(_gelu(_dot(h, win_ref[:, o_gv:o_ga])), ggv_ref[...])
    lane_c = lax.broadcasted_iota(jnp.int32, (CHUNK, LANES), 1)
    mixed_rows = []
    for ci in range(x.shape[0] // CHUNK):
        cols = []
        for j in range(GMLP_WIDTH // LANES):
            vs = vg[ci * CHUNK:(ci + 1) * CHUNK, j * LANES:(j + 1) * LANES].astype(_BF16)
            zero = jnp.zeros_like(vs)
            rhs = jnp.concatenate([jnp.where(lane_c < GMLP_GROUP_DIM, vs, zero),
                                   jnp.where(lane_c < GMLP_GROUP_DIM, zero, vs)], axis=0)
            cols.append(_dot(wsp_ref[j], rhs))
        mixed_rows.append(jnp.concatenate(cols, axis=1) + bsp_ref[...])
    sg = (u * jnp.concatenate(mixed_rows, axis=0)).astype(_BF16)
    gate_b = jax.nn.sigmoid(_dot(h, win_ref[:, o_gb:o_gb + D_MODEL]))
    mb_ref[...] = (gate_b * _dot(sg, wbg_ref[...])).astype(_BF16)


def _attn_kernel(q_ref, k_ref, vt_ref, kn_ref, o_ref, qs_scr, c_scr, acc_scr, p_scr, *, n_key_tiles):
    tq = q_ref.shape[0]
    lane = lax.broadcasted_iota(jnp.int32, (tq, LANES), 1)
    for hq in range(Q_PER_KV):
        pair = q_ref[:, (hq // 2) * LANES:(hq // 2 + 1) * LANES]
        keep = (lane < HEAD_DIM) if hq % 2 == 0 else (lane >= HEAD_DIM)
        qs_scr[hq * tq:(hq + 1) * tq, :] = jnp.where(keep, pair, jnp.zeros_like(pair))
    acc_scr[...] = jnp.zeros(acc_scr.shape, _F32)

    def scores(kt, c):
        start = pl.multiple_of(kt * KEY_TILE, KEY_TILE)
        return lax.dot_general(k_ref[pl.ds(start, KEY_TILE), :], qs_scr[c * tq:(c + 1) * tq, :],
                               (((1,), (1,)), ((), ())), preferred_element_type=_F32)

    qf = qs_scr[...].astype(_F32)
    q_norm2 = lax.dot_general(jnp.ones((8, LANES), _BF16), (qf * qf).astype(_BF16), (((1,), (1,)), ((), ())),
                              preferred_element_type=_F32)
    k_norm2 = jnp.max(kn_ref[...], axis=0)
    lane8 = lax.broadcasted_iota(jnp.int32, k_norm2.shape, 1)
    head_lo = pl.program_id(1) * HEAD_DIM
    this_head = (lane8 >= head_lo) & (lane8 < head_lo + HEAD_DIM)
    k_max2 = jnp.max(jnp.where(this_head, k_norm2, 0.0), axis=1, keepdims=True)
    bound = jnp.sqrt(q_norm2 * k_max2) * SHIFT_MARGIN
    c_scr[...] = bound[0:1, :]

    @pl.when(jnp.max(bound) > SAFE_SHIFT_MAX)
    def _():
        def running_max(kt, m):
            tile_max = [jnp.max(scores(kt, c), axis=0, keepdims=True) for c in range(Q_PER_KV)]
            return jnp.maximum(m, jnp.concatenate(tile_max, axis=1))
        c_scr[...] = lax.fori_loop(0, n_key_tiles, running_max, jnp.full(c_scr.shape, NEG_BIG, _F32))

    def probabilities(kt, slot):
        for c in range(Q_PER_KV):
            cols = slice(c * tq, (c + 1) * tq)
            p_scr[slot, :, cols] = jnp.exp2(scores(kt, c) - c_scr[:, cols]).astype(_BF16)

    def accumulate(kt, slot):
        for c in range(Q_PER_KV):
            cols = slice(c * tq, (c + 1) * tq)
            acc_scr[:, cols] += _dot(vt_ref[kt], p_scr[slot, :, cols])

    probabilities(0, 0)

    def key_tile_group(i, carry):
        for j in range(KEY_TILES_PER_ITER):
            kt = KEY_TILES_PER_ITER * i + j
            probabilities(kt + 1 if j + 1 < KEY_TILES_PER_ITER else jnp.minimum(kt + 1, n_key_tiles - 1),
                          (j + 1) % 2)
            accumulate(kt, j % 2)
        return carry

    assert n_key_tiles % KEY_TILES_PER_ITER == 0 and KEY_TILES_PER_ITER % 2 == 0
    lax.fori_loop(0, n_key_tiles // KEY_TILES_PER_ITER, key_tile_group, 0)

    acc = acc_scr[...]
    out_t = acc[0:HEAD_DIM, :] / acc[HEAD_DIM:HEAD_DIM + 1, :]
    for j in range(Q_PER_KV // 2):
        pair_t = jnp.concatenate([out_t[:, (2 * j) * tq:(2 * j + 1) * tq],
                                  out_t[:, (2 * j + 1) * tq:(2 * j + 2) * tq]], axis=0)
        o_ref[:, j * LANES:(j + 1) * LANES] = pair_t.T.astype(_BF16)


def _post_kernel(a_ref, sga_ref, mb_ref, x1_ref, p_ref, wba_ref, wout_ref, g2_ref, wgu_ref, wdn_ref, gple_ref,
                 wpg_ref, wple_ref, gfin_ref, y_ref, act_scr):
    merged = sga_ref[...].astype(_F32) * _dot(a_ref[...], wba_ref[...]) + mb_ref[...].astype(_F32)
    x2 = x1_ref[...] + _dot(merged.astype(_BF16), wout_ref[...])
    x3 = x2 + 0.5 * _swiglu(_rms(x2, g2_ref[...]).astype(_BF16), wgu_ref, wdn_ref, act_scr)
    gate = jax.nn.sigmoid(_dot(_rms(x3, gple_ref[...]).astype(_BF16), wpg_ref[...]))
    x4 = x3 + gate * _dot(p_ref[...].astype(_BF16), wple_ref[...])
    y_ref[...] = _rms(x4, gfin_ref[...])


def _resident(shape):
    nd = len(shape)
    return pl.BlockSpec(shape, lambda *_: (0,) * nd, pipeline_mode=pl.Buffered(1))


def _rows(width, tile=ROW_TILE):
    return pl.BlockSpec((tile, width), lambda i: (i, 0))


def _rope_tables(seq):
    rows = seq // GRID_W
    row = jnp.repeat(jnp.arange(rows, dtype=_F32), GRID_W)
    col = jnp.tile(jnp.arange(GRID_W, dtype=_F32), rows)
    inv = jnp.power(jnp.float32(ROPE_THETA), -jnp.arange(ROPE_AXIS_FREQS, dtype=_F32) / ROPE_AXIS_FREQS)
    ang = jnp.stack([row[:, None] * inv, col[:, None] * inv], axis=1)
    cos = jnp.cos(ang)
    sin = jnp.sin(ang)
    cos_h = jnp.concatenate([cos[:, 0], cos[:, 0], cos[:, 1], cos[:, 1]], axis=1)
    sin_h = jnp.concatenate([-sin[:, 0], sin[:, 0], -sin[:, 1], sin[:, 1]], axis=1)
    return jnp.tile(cos_h, (1, LANES // HEAD_DIM)), jnp.tile(sin_h, (1, LANES // HEAD_DIM))


def _block_diag_ones(width):
    idx = jnp.arange(width) // HEAD_DIM
    return (idx[:, None] == idx[None, :]).astype(_BF16)


def _trunk(x, p, prm, cos_t, sin_t):
    batch, seq, _ = x.shape
    rows = batch * seq
    n_tiles = rows // ROW_TILE
    seq_tiles = seq // ROW_TILE
    x2d = x.reshape(rows, D_MODEL)
    p2d = p.reshape(rows, PLE_DIM)
    cparams = pltpu.CompilerParams(dimension_semantics=("arbitrary",), vmem_limit_bytes=VMEM_LIMIT)

    pos_spec = pl.BlockSpec((ROW_TILE, LANES), lambda i: (i % seq_tiles, 0))
    x1, q, k2, vt, kn, sga, mb = pl.pallas_call(
        _pre_kernel,
        grid=(n_tiles,),
        in_specs=[_rows(D_MODEL), pos_spec, pos_spec,
                  _resident((1, D_MODEL)), _resident((D_MODEL, 2 * D_FF)), _resident((D_FF, D_MODEL)),
                  _resident((1, D_MODEL)), _resident(prm["w_in"].shape),
                  _resident((1, ATTN_WIDTH)), _resident((1, KV_WIDTH)), _resident((1, GMLP_WIDTH)),
                  _resident(prm["w_sp"].shape), _resident((CHUNK, GMLP_WIDTH)), _resident((GMLP_WIDTH, D_MODEL)),
                  _resident((ATTN_WIDTH, ATTN_WIDTH)), _resident((KV_WIDTH, KV_WIDTH))],
        out_specs=[_rows(D_MODEL), _rows(ATTN_WIDTH),
                   pl.BlockSpec((N_KV_HEADS, ROW_TILE, LANES), lambda i: (0, i, 0)),
                   pl.BlockSpec((N_KV_HEADS, ROW_TILE // KEY_TILE, V_ROWS, KEY_TILE), lambda i: (0, i, 0, 0)),
                   pl.BlockSpec((1, 8, LANES), lambda i: (i, 0, 0)),
                   _rows(D_MODEL), _rows(D_MODEL)],
        out_shape=[jax.ShapeDtypeStruct((rows, D_MODEL), _F32),
                   jax.ShapeDtypeStruct((rows, ATTN_WIDTH), _BF16),
                   jax.ShapeDtypeStruct((N_KV_HEADS, rows, LANES), _BF16),
                   jax.ShapeDtypeStruct((N_KV_HEADS, rows // KEY_TILE, V_ROWS, KEY_TILE), _BF16),
                   jax.ShapeDtypeStruct((n_tiles, 8, LANES), _F32),
                   jax.ShapeDtypeStruct((rows, D_MODEL), _BF16),
                   jax.ShapeDtypeStruct((rows, D_MODEL), _BF16)],
        scratch_shapes=[pltpu.VMEM((ROW_TILE, D_FF), _BF16)],
        compiler_params=cparams,
        name="pre",
    )(x2d, cos_t, sin_t, prm["g_ffn1"], prm["w_gu1"], prm["w_dn1"], prm["g_mix"], prm["w_in"],
      prm["g_q"], prm["g_k"], prm["g_gv"], prm["w_sp"], prm["b_sp"], prm["w_bg"], prm["e_q"], prm["e_k"])

    q_tiles = seq // Q_TILE
    a = pl.pallas_call(
        functools.partial(_attn_kernel, n_key_tiles=seq // KEY_TILE),
        grid=(batch, N_KV_HEADS, q_tiles),
        in_specs=[pl.BlockSpec((Q_TILE, 2 * LANES), lambda b, kh, qi: (b * q_tiles + qi, kh)),
                  pl.BlockSpec((None, seq, LANES), lambda b, kh, qi: (kh, b, 0)),
                  pl.BlockSpec((None, seq // KEY_TILE, V_ROWS, KEY_TILE), lambda b, kh, qi: (kh, b, 0, 0)),
                  pl.BlockSpec((seq_tiles, 8, LANES), lambda b, kh, qi: (b, 0, 0))],
        out_specs=pl.BlockSpec((Q_TILE, 2 * LANES), lambda b, kh, qi: (b * q_tiles + qi, kh)),
        out_shape=jax.ShapeDtypeStruct((rows, ATTN_WIDTH), _BF16),
        scratch_shapes=[pltpu.VMEM((Q_PER_KV * Q_TILE, LANES), _BF16),
                        pltpu.VMEM((1, Q_PER_KV * Q_TILE), _F32),
                        pltpu.VMEM((V_ROWS, Q_PER_KV * Q_TILE), _F32),
                        pltpu.VMEM((2, KEY_TILE, Q_PER_KV * Q_TILE), _BF16)],
        compiler_params=pltpu.CompilerParams(dimension_semantics=("arbitrary",) * 3, vmem_limit_bytes=VMEM_LIMIT),
        name="attn",
    )(q, k2, vt, kn)

    y = pl.pallas_call(
        _post_kernel,
        grid=(n_tiles,),
        in_specs=[_rows(ATTN_WIDTH), _rows(D_MODEL), _rows(D_MODEL), _rows(D_MODEL), _rows(PLE_DIM),
                  _resident((ATTN_WIDTH, D_MODEL)), _resident((D_MODEL, D_MODEL)),
                  _resident((1, D_MODEL)), _resident((D_MODEL, 2 * D_FF)), _resident((D_FF, D_MODEL)),
                  _resident((1, D_MODEL)), _resident((D_MODEL, D_MODEL)), _resident((PLE_DIM, D_MODEL)),
                  _resident((1, D_MODEL))],
        out_specs=_rows(D_MODEL),
        out_shape=jax.ShapeDtypeStruct((rows, D_MODEL), _F32),
        scratch_shapes=[pltpu.VMEM((ROW_TILE, D_FF), _BF16)],
        compiler_params=cparams,
        name="post",
    )(a, sga, mb, x1, p2d, prm["w_ba"], prm["w_out"], prm["g_ffn2"], prm["w_gu2"], prm["w_dn2"],
      prm["g_ple"], prm["w_pg"], prm["w_ple"], prm["g_final"])
    return y.reshape(batch, seq, D_MODEL)


def kernel(x_prompt, x_sample, p_prompt, p_sample, g_ffn1, w_ffn1_gu, w_ffn1_down, g_mix, w_in, g_q, g_k, g_gmlp_v, w_spatial, b_spatial, w_branch_attn, w_branch_gmlp, w_out, g_ffn2, w_ffn2_gu, w_ffn2_down, g_ple, w_ple_gate, w_ple, g_final):
    assert g_ffn1.shape[0] == 1, "the post kernel fuses the final norm into the single layer"
    cos_t, sin_t = _rope_tables(max(x_prompt.shape[1], x_sample.shape[1]))
    w_sp = w_spatial[0].reshape(GMLP_GROUPS // 2, 2, CHUNK, CHUNK).transpose(0, 2, 1, 3)
    prm = dict(
        g_ffn1=g_ffn1[0][None], w_gu1=w_ffn1_gu[0].astype(_BF16), w_dn1=w_ffn1_down[0].astype(_BF16),
        g_mix=g_mix[0][None], w_in=w_in[0].astype(_BF16),
        g_q=jnp.tile(g_q[0], N_HEADS)[None], g_k=jnp.tile(g_k[0], N_KV_HEADS)[None],
        g_gv=g_gmlp_v[0][None],
        w_sp=w_sp.reshape(GMLP_GROUPS // 2, CHUNK, 2 * CHUNK).astype(_BF16),
        b_sp=jnp.repeat(b_spatial[0].T, GMLP_GROUP_DIM, axis=1),
        w_bg=w_branch_gmlp[0].astype(_BF16), w_ba=w_branch_attn[0].astype(_BF16),
        w_out=w_out[0].astype(_BF16), g_ffn2=g_ffn2[0][None], w_gu2=w_ffn2_gu[0].astype(_BF16),
        w_dn2=w_ffn2_down[0].astype(_BF16), g_ple=g_ple[0][None], w_pg=w_ple_gate[0].astype(_BF16),
        w_ple=w_ple[0].astype(_BF16), g_final=g_final[None],
        e_q=_block_diag_ones(ATTN_WIDTH), e_k=_block_diag_ones(KV_WIDTH),
    )
    return (_trunk(x_prompt, p_prompt[0], prm, cos_t, sin_t), _trunk(x_sample, p_sample[0], prm, cos_t, sin_t))
```

```python
import functools

import jax
import jax.numpy as jnp
from jax import lax
from jax.experimental import pallas as pl
from jax.experimental.pallas import tpu as pltpu

D_MODEL = 1024
N_HEADS = 8
N_KV_HEADS = 2
HEAD_DIM = 64
Q_PER_KV = N_HEADS // N_KV_HEADS
ATTN_WIDTH = N_HEADS * HEAD_DIM
KV_WIDTH = N_KV_HEADS * HEAD_DIM
GMLP_GROUPS = 8
GMLP_GROUP_DIM = 64
GMLP_WIDTH = GMLP_GROUPS * GMLP_GROUP_DIM
CHUNK = 128
GRID_W = 64
ROPE_THETA = 10000.0
ROPE_AXIS_FREQS = HEAD_DIM // 4
D_FF = 2816
PLE_DIM = 256
EPS = 1e-6

LANES = 128
ROW_TILE = 512
KEY_TILE = 512
Q_TILE = 256
KEY_TILES_PER_ITER = 8
FF_CHUNK = 256
N_FF_CHUNKS = D_FF // FF_CHUNK
V_ROWS = 2 * HEAD_DIM
NEG_BIG = -1e30
LOG2_E = 1.4426950408889634
SAFE_SHIFT_MAX = 50.0
SHIFT_MARGIN = 1.01
VMEM_LIMIT = 56 * 1024 * 1024

_F32 = jnp.float32
_BF16 = jnp.bfloat16


def _dot(a, b):
    return jnp.dot(a, b, preferred_element_type=_F32)


def _rms(x, g):
    return x * lax.rsqrt(jnp.mean(x * x, axis=-1, keepdims=True) + EPS) * g


def _gelu(x):
    return 0.5 * x * (1.0 + lax.erf(x * (2.0 ** -0.5)))


def _group_sumsq(t, ones_blockdiag):
    sq = t * t
    hi = sq.astype(_BF16)
    lo = (sq - hi.astype(_F32)).astype(_BF16)
    return _dot(hi, ones_blockdiag) + _dot(lo, ones_blockdiag)


def _rope(t, cos, sin_signed):
    width = t.shape[-1]
    lane = lax.broadcasted_iota(jnp.int32, t.shape, 1)
    first_half = (lane & (2 * ROPE_AXIS_FREQS - 1)) < ROPE_AXIS_FREQS
    partner = jnp.where(first_half, pltpu.roll(t, width - ROPE_AXIS_FREQS, 1), pltpu.roll(t, ROPE_AXIS_FREQS, 1))
    return t * cos + partner * sin_signed


def _swiglu(xn_bf16, wgu_ref, wdn_ref, act_scr):
    for c in range(N_FF_CHUNKS):
        g = _dot(xn_bf16, wgu_ref[:, c * FF_CHUNK:(c + 1) * FF_CHUNK])
        u = _dot(xn_bf16, wgu_ref[:, D_FF + c * FF_CHUNK:D_FF + (c + 1) * FF_CHUNK])
        act_scr[:, c * FF_CHUNK:(c + 1) * FF_CHUNK] = (g * jax.nn.sigmoid(g) * u).astype(_BF16)
    return _dot(act_scr[...], wdn_ref[...])


def _pre_kernel(x_ref, cos_ref, sin_ref, g1_ref, wgu_ref, wdn_ref, gmix_ref, win_ref, gq_ref, gk_ref, ggv_ref,
                wsp_ref, bsp_ref, wbg_ref, eq_ref, ek_ref,
                x1_ref, q_ref, k2_ref, vt_ref, kn_ref, sga_ref, mb_ref, act_scr):
    x = x_ref[...]
    x1 = x + 0.5 * _swiglu(_rms(x, g1_ref[...]).astype(_BF16), wgu_ref, wdn_ref, act_scr)
    x1_ref[...] = x1
    h = _rms(x1, gmix_ref[...]).astype(_BF16)

    cos = cos_ref[...]
    sin = sin_ref[...]

    q = _dot(h, win_ref[:, 0:ATTN_WIDTH])
    q = q * lax.rsqrt(_group_sumsq(q, eq_ref[...]) * (1.0 / HEAD_DIM) + EPS) * gq_ref[...]
    q = _rope(q, jnp.concatenate([cos] * (ATTN_WIDTH // LANES), axis=1),
              jnp.concatenate([sin] * (ATTN_WIDTH // LANES), axis=1))
    q_ref[...] = (q * (HEAD_DIM ** -0.5 * LOG2_E)).astype(_BF16)

    kv = _dot(h, win_ref[:, ATTN_WIDTH:ATTN_WIDTH + 2 * KV_WIDTH])
    k = kv[:, :KV_WIDTH]
    v = kv[:, KV_WIDTH:]
    k = k * lax.rsqrt(_group_sumsq(k, ek_ref[...]) * (1.0 / HEAD_DIM) + EPS) * gk_ref[...]
    k = _rope(k, cos, sin)
    kn_ref[0] = jnp.broadcast_to(jnp.max(_group_sumsq(k, ek_ref[...]), axis=0, keepdims=True), kn_ref.shape[1:])
    k_swapped = pltpu.roll(k, HEAD_DIM, 1)
    lane = lax.broadcasted_iota(jnp.int32, k.shape, 1)
    k2_ref[0] = jnp.where(lane < HEAD_DIM, k, k_swapped).astype(_BF16)
    k2_ref[1] = jnp.where(lane < HEAD_DIM, k_swapped, k).astype(_BF16)

    vt = v.T.astype(_BF16)
    ones = jnp.ones((V_ROWS - HEAD_DIM, KEY_TILE), _BF16)
    for kh in range(N_KV_HEADS):
        for t in range(vt.shape[1] // KEY_TILE):
            vt_ref[kh, t, 0:HEAD_DIM, :] = vt[kh * HEAD_DIM:(kh + 1) * HEAD_DIM, t * KEY_TILE:(t + 1) * KEY_TILE]
            vt_ref[kh, t, HEAD_DIM:V_ROWS, :] = ones

    o_gu = ATTN_WIDTH + 2 * KV_WIDTH
    o_gv = o_gu + GMLP_WIDTH
    o_ga = o_gv + GMLP_WIDTH
    o_gb = o_ga + D_MODEL
    sga_ref[...] = jax.nn.sigmoid(_dot(h, win_ref[:, o_ga:o_gb])).astype(_BF16)

    u = _gelu(_dot(h, win_ref[:, o_gu:o_gv]))
    vg = _rms(_gelu(_dot(h, win_ref[:, o_gv:o_ga])), ggv_ref[...])
    lane_c = lax.broadcasted_iota(jnp.int32, (CHUNK, LANES), 1)
    mixed_rows = []
    for ci in range(x.shape[0] // CHUNK):
        cols = []
        for j in range(GMLP_WIDTH // LANES):
            vs = vg[ci * CHUNK:(ci + 1) * CHUNK, j * LANES:(j + 1) * LANES].astype(_BF16)
            zero = jnp.zeros_like(vs)
            rhs = jnp.concatenate([jnp.where(lane_c < GMLP_GROUP_DIM, vs, zero),
                                   jnp.where(lane_c < GMLP_GROUP_DIM, zero, vs)], axis=0)
            cols.append(_dot(wsp_ref[j], rhs))
        mixed_rows.append(jnp.concatenate(cols, axis=1) + bsp_ref[...])
    sg = (u * jnp.concatenate(mixed_rows, axis=0)).astype(_BF16)
    gate_b = jax.nn.sigmoid(_dot(h, win_ref[:, o_gb:o_gb + D_MODEL]))
    mb_ref[...] = (gate_b * _dot(sg, wbg_ref[...])).astype(_BF16)


def _attn_kernel(q_ref, k_ref, vt_ref, kn_ref, o_ref, qs_scr, c_scr, acc_scr, p_scr, *, n_key_tiles):
    tq = q_ref.shape[0]
    lane = lax.broadcasted_iota(jnp.int32, (tq, LANES), 1)
    for hq in range(Q_PER_KV):
        pair = q_ref[:, (hq // 2) * LANES:(hq // 2 + 1) * LANES]
        keep = (lane < HEAD_DIM) if hq % 2 == 0 else (lane >= HEAD_DIM)
        qs_scr[hq * tq:(hq + 1) * tq, :] = jnp.where(keep, pair, jnp.zeros_like(pair))
    acc_scr[...] = jnp.zeros(acc_scr.shape, _F32)

    def scores(kt, c):
        start = pl.multiple_of(kt * KEY_TILE, KEY_TILE)
        return lax.dot_general(k_ref[pl.ds(start, KEY_TILE), :], qs_scr[c * tq:(c + 1) * tq, :],
                               (((1,), (1,)), ((), ())), preferred_element_type=_F32)

    qf = qs_scr[...].astype(_F32)
    q_norm2 = lax.dot_general(jnp.ones((8, LANES), _BF16), (qf * qf).astype(_BF16), (((1,), (1,)), ((), ())),
                              preferred_element_type=_F32)
    k_norm2 = jnp.max(kn_ref[...], axis=0)
    lane8 = lax.broadcasted_iota(jnp.int32, k_norm2.shape, 1)
    head_lo = pl.program_id(1) * HEAD_DIM
    this_head = (lane8 >= head_lo) & (lane8 < head_lo + HEAD_DIM)
    k_max2 = jnp.max(jnp.where(this_head, k_norm2, 0.0), axis=1, keepdims=True)
    bound = jnp.sqrt(q_norm2 * k_max2) * SHIFT_MARGIN
    c_scr[...] = bound[0:1, :]

    @pl.when(jnp.max(bound) > SAFE_SHIFT_MAX)
    def _():
        def running_max(kt, m):
            tile_max = [jnp.max(scores(kt, c), axis=0, keepdims=True) for c in range(Q_PER_KV)]
            return jnp.maximum(m, jnp.concatenate(tile_max, axis=1))
        c_scr[...] = lax.fori_loop(0, n_key_tiles, running_max, jnp.full(c_scr.shape, NEG_BIG, _F32))

    def probabilities(kt, slot):
        for c in range(Q_PER_KV):
            cols = slice(c * tq, (c + 1) * tq)
            p_scr[slot, :, cols] = jnp.exp2(scores(kt, c) - c_scr[:, cols]).astype(_BF16)

    def accumulate(kt, slot):
        for c in range(Q_PER_KV):
            cols = slice(c * tq, (c + 1) * tq)
            acc_scr[:, cols] += _dot(vt_ref[kt], p_scr[slot, :, cols])

    probabilities(0, 0)

    unroll = min(KEY_TILES_PER_ITER, n_key_tiles // 2)

    def key_tile_group(i, carry):
        for j in range(unroll):
            kt = unroll * i + j
            probabilities(kt + 1 if j + 1 < unroll else jnp.minimum(kt + 1, n_key_tiles - 1), (j + 1) % 2)
            accumulate(kt, j % 2)
        return carry

    assert n_key_tiles % unroll == 0 and unroll % 2 == 0
    lax.fori_loop(0, n_key_tiles // unroll, key_tile_group, 0)

    acc = acc_scr[...]
    out_t = acc[0:HEAD_DIM, :] / acc[HEAD_DIM:HEAD_DIM + 1, :]
    for j in range(Q_PER_KV // 2):
        pair_t = jnp.concatenate([out_t[:, (2 * j) * tq:(2 * j + 1) * tq],
                                  out_t[:, (2 * j + 1) * tq:(2 * j + 2) * tq]], axis=0)
        o_ref[:, j * LANES:(j + 1) * LANES] = pair_t.T.astype(_BF16)


def _post_kernel(a_ref, sga_ref, mb_ref, x1_ref, p_ref, wba_ref, wout_ref, g2_ref, wgu_ref, wdn_ref, gple_ref,
                 wpg_ref, wple_ref, gfin_ref, y_ref, act_scr):
    merged = sga_ref[...].astype(_F32) * _dot(a_ref[...], wba_ref[...]) + mb_ref[...].astype(_F32)
    x2 = x1_ref[...] + _dot(merged.astype(_BF16), wout_ref[...])
    x3 = x2 + 0.5 * _swiglu(_rms(x2, g2_ref[...]).astype(_BF16), wgu_ref, wdn_ref, act_scr)
    gate = jax.nn.sigmoid(_dot(_rms(x3, gple_ref[...]).astype(_BF16), wpg_ref[...]))
    x4 = x3 + gate * _dot(p_ref[...].astype(_BF16), wple_ref[...])
    y_ref[...] = _rms(x4, gfin_ref[...])


def _resident(shape):
    nd = len(shape)
    return pl.BlockSpec(shape, lambda *_: (0,) * nd, pipeline_mode=pl.Buffered(1))


def _rows(width, tile=ROW_TILE):
    return pl.BlockSpec((tile, width), lambda i: (i, 0))


def _rope_tables(seq):
    rows = seq // GRID_W
    row = jnp.repeat(jnp.arange(rows, dtype=_F32), GRID_W)
    col = jnp.tile(jnp.arange(GRID_W, dtype=_F32), rows)
    inv = jnp.power(jnp.float32(ROPE_THETA), -jnp.arange(ROPE_AXIS_FREQS, dtype=_F32) / ROPE_AXIS_FREQS)
    ang = jnp.stack([row[:, None] * inv, col[:, None] * inv], axis=1)
    cos = jnp.cos(ang)
    sin = jnp.sin(ang)
    cos_h = jnp.concatenate([cos[:, 0], cos[:, 0], cos[:, 1], cos[:, 1]], axis=1)
    sin_h = jnp.concatenate([-sin[:, 0], sin[:, 0], -sin[:, 1], sin[:, 1]], axis=1)
    return jnp.tile(cos_h, (1, LANES // HEAD_DIM)), jnp.tile(sin_h, (1, LANES // HEAD_DIM))


def _block_diag_ones(width):
    idx = jnp.arange(width) // HEAD_DIM
    return (idx[:, None] == idx[None, :]).astype(_BF16)


def _trunk(x, p, prm, cos_t, sin_t):
    batch, seq, _ = x.shape
    rows = batch * seq
    n_tiles = rows // ROW_TILE
    seq_tiles = seq // ROW_TILE
    x2d = x.reshape(rows, D_MODEL)
    p2d = p.reshape(rows, PLE_DIM)
    cparams = pltpu.CompilerParams(dimension_semantics=("arbitrary",), vmem_limit_bytes=VMEM_LIMIT)

    pos_spec = pl.BlockSpec((ROW_TILE, LANES), lambda i: (i % seq_tiles, 0))
    x1, q, k2, vt, kn, sga, mb = pl.pallas_call(
        _pre_kernel,
        grid=(n_tiles,),
        in_specs=[_rows(D_MODEL), pos_spec, pos_spec,
                  _resident((1, D_MODEL)), _resident((D_MODEL, 2 * D_FF)), _resident((D_FF, D_MODEL)),
                  _resident((1, D_MODEL)), _resident(prm["w_in"].shape),
                  _resident((1, ATTN_WIDTH)), _resident((1, KV_WIDTH)), _resident((1, GMLP_WIDTH)),
                  _resident(prm["w_sp"].shape), _resident((CHUNK, GMLP_WIDTH)), _resident((GMLP_WIDTH, D_MODEL)),
                  _resident((ATTN_WIDTH, ATTN_WIDTH)), _resident((KV_WIDTH, KV_WIDTH))],
        out_specs=[_rows(D_MODEL), _rows(ATTN_WIDTH),
                   pl.BlockSpec((N_KV_HEADS, ROW_TILE, LANES), lambda i: (0, i, 0)),
                   pl.BlockSpec((N_KV_HEADS, ROW_TILE // KEY_TILE, V_ROWS, KEY_TILE), lambda i: (0, i, 0, 0)),
                   pl.BlockSpec((1, 8, LANES), lambda i: (i, 0, 0)),
                   _rows(D_MODEL), _rows(D_MODEL)],
        out_shape=[jax.ShapeDtypeStruct((rows, D_MODEL), _F32),
                   jax.ShapeDtypeStruct((rows, ATTN_WIDTH), _BF16),
                   jax.ShapeDtypeStruct((N_KV_HEADS, rows, LANES), _BF16),
                   jax.ShapeDtypeStruct((N_KV_HEADS, rows // KEY_TILE, V_ROWS, KEY_TILE), _BF16),
                   jax.ShapeDtypeStruct((n_tiles, 8, LANES), _F32),
                   jax.ShapeDtypeStruct((rows, D_MODEL), _BF16),
                   jax.ShapeDtypeStruct((rows, D_MODEL), _BF16)],
        scratch_shapes=[pltpu.VMEM((ROW_TILE, D_FF), _BF16)],
        compiler_params=cparams,
        name="pre",
    )(x2d, cos_t, sin_t, prm["g_ffn1"], prm["w_gu1"], prm["w_dn1"], prm["g_mix"], prm["w_in"],
      prm["g_q"], prm["g_k"], prm["g_gv"], prm["w_sp"], prm["b_sp"], prm["w_bg"], prm["e_q"], prm["e_k"])

    q_tiles = seq // Q_TILE
    a = pl.pallas_call(
        functools.partial(_attn_kernel, n_key_tiles=seq // KEY_TILE),
        grid=(batch, N_KV_HEADS, q_tiles),
        in_specs=[pl.BlockSpec((Q_TILE, 2 * LANES), lambda b, kh, qi: (b * q_tiles + qi, kh)),
                  pl.BlockSpec((None, seq, LANES), lambda b, kh, qi: (kh, b, 0)),
                  pl.BlockSpec((None, seq // KEY_TILE, V_ROWS, KEY_TILE), lambda b, kh, qi: (kh, b, 0, 0)),
                  pl.BlockSpec((seq_tiles, 8, LANES), lambda b, kh, qi: (b, 0, 0))],
        out_specs=pl.BlockSpec((Q_TILE, 2 * LANES), lambda b, kh, qi: (b * q_tiles + qi, kh)),
        out_shape=jax.ShapeDtypeStruct((rows, ATTN_WIDTH), _BF16),
        scratch_shapes=[pltpu.VMEM((Q_PER_KV * Q_TILE, LANES), _BF16),
                        pltpu.VMEM((1, Q_PER_KV * Q_TILE), _F32),
                        pltpu.VMEM((V_ROWS, Q_PER_KV * Q_TILE), _F32),
                        pltpu.VMEM((2, KEY_TILE, Q_PER_KV * Q_TILE), _BF16)],
        compiler_params=pltpu.CompilerParams(dimension_semantics=("arbitrary",) * 3, vmem_limit_bytes=VMEM_LIMIT),
        name="attn",
    )(q, k2, vt, kn)

    y = pl.pallas_call(
        _post_kernel,
        grid=(n_tiles,),
        in_specs=[_rows(ATTN_WIDTH), _rows(D_MODEL), _rows(D_MODEL), _rows(D_MODEL), _rows(PLE_DIM),
                  _resident((ATTN_WIDTH, D_MODEL)), _resident((D_MODEL, D_MODEL)),
                  _resident((1, D_MODEL)), _resident((D_MODEL, 2 * D_FF)), _resident((D_FF, D_MODEL)),
                  _resident((1, D_MODEL)), _resident((D_MODEL, D_MODEL)), _resident((PLE_DIM, D_MODEL)),
                  _resident((1, D_MODEL))],
        out_specs=_rows(D_MODEL),
        out_shape=jax.ShapeDtypeStruct((rows, D_MODEL), _F32),
        scratch_shapes=[pltpu.VMEM((ROW_TILE, D_FF), _BF16)],
        compiler_params=cparams,
        name="post",
    )(a, sga, mb, x1, p2d, prm["w_ba"], prm["w_out"], prm["g_ffn2"], prm["w_gu2"], prm["w_dn2"],
      prm["g_ple"], prm["w_pg"], prm["w_ple"], prm["g_final"])
    return y.reshape(batch, seq, D_MODEL)


def kernel(x_prompt, x_sample, p_prompt, p_sample, g_ffn1, w_ffn1_gu, w_ffn1_down, g_mix, w_in, g_q, g_k, g_gmlp_v, w_spatial, b_spatial, w_branch_attn, w_branch_gmlp, w_out, g_ffn2, w_ffn2_gu, w_ffn2_down, g_ple, w_ple_gate, w_ple, g_final):
    assert g_ffn1.shape[0] == 1, "the post kernel fuses the final norm into the single layer"
    cos_t, sin_t = _rope_tables(max(x_prompt.shape[1], x_sample.shape[1]))
    w_sp = w_spatial[0].reshape(GMLP_GROUPS // 2, 2, CHUNK, CHUNK).transpose(0, 2, 1, 3)
    prm = dict(
        g_ffn1=g_ffn1[0][None], w_gu1=w_ffn1_gu[0].astype(_BF16), w_dn1=w_ffn1_down[0].astype(_BF16),
        g_mix=g_mix[0][None], w_in=w_in[0].astype(_BF16),
        g_q=jnp.tile(g_q[0], N_HEADS)[None], g_k=jnp.tile(g_k[0], N_KV_HEADS)[None],
        g_gv=g_gmlp_v[0][None],
        w_sp=w_sp.reshape(GMLP_GROUPS // 2, CHUNK, 2 * CHUNK).astype(_BF16),
        b_sp=jnp.repeat(b_spatial[0].T, GMLP_GROUP_DIM, axis=1),
        w_bg=w_branch_gmlp[0].astype(_BF16), w_ba=w_branch_attn[0].astype(_BF16),
        w_out=w_out[0].astype(_BF16), g_ffn2=g_ffn2[0][None], w_gu2=w_ffn2_gu[0].astype(_BF16),
        w_dn2=w_ffn2_down[0].astype(_BF16), g_ple=g_ple[0][None], w_pg=w_ple_gate[0].astype(_BF16),
        w_ple=w_ple[0].astype(_BF16), g_final=g_final[None],
        e_q=_block_diag_ones(ATTN_WIDTH), e_k=_block_diag_ones(KV_WIDTH),
    )
    return (_trunk(x_prompt, p_prompt[0], prm, cos_t, sin_t), _trunk(x_sample, p_sample[0], prm, cos_t, sin_t))
```

```python
import functools

import jax
import jax.numpy as jnp
from jax import lax
from jax.experimental import pallas as pl
from jax.experimental.pallas import tpu as pltpu

D_MODEL = 1024
N_HEADS = 8
N_KV_HEADS = 2
HEAD_DIM = 64
Q_PER_KV = N_HEADS // N_KV_HEADS
ATTN_WIDTH = N_HEADS * HEAD_DIM
KV_WIDTH = N_KV_HEADS * HEAD_DIM
GMLP_GROUPS = 8
GMLP_GROUP_DIM = 64
GMLP_WIDTH = GMLP_GROUPS * GMLP_GROUP_DIM
CHUNK = 128
GRID_W = 64
ROPE_THETA = 10000.0
ROPE_AXIS_FREQS = HEAD_DIM // 4
D_FF = 2816
PLE_DIM = 256
EPS = 1e-6

LANES = 128
ROW_TILE = 512
SUB_ROWS = 256
N_SUB = ROW_TILE // SUB_ROWS
KEY_TILE = 512
Q_TILE = 256
KEY_TILES_PER_ITER = 8
FF_CHUNK = 256
N_FF_CHUNKS = D_FF // FF_CHUNK
V_ROWS = 2 * HEAD_DIM
NEG_BIG = -1e30
LOG2_E = 1.4426950408889634
SAFE_SHIFT_MAX = 50.0
SHIFT_MARGIN = 1.01
VMEM_LIMIT = 56 * 1024 * 1024

_F32 = jnp.float32
_BF16 = jnp.bfloat16


def _dot(a, b):
    return jnp.dot(a, b, preferred_element_type=_F32)


def _rms(x, g):
    return x * lax.rsqrt(jnp.mean(x * x, axis=-1, keepdims=True) + EPS) * g


def _gelu(x):
    return 0.5 * x * (1.0 + lax.erf(x * (2.0 ** -0.5)))


def _group_sumsq(t, ones_blockdiag):
    sq = t * t
    hi = sq.astype(_BF16)
    lo = (sq - hi.astype(_F32)).astype(_BF16)
    return _dot(hi, ones_blockdiag) + _dot(lo, ones_blockdiag)


def _rope(t, cos, sin_signed):
    width = t.shape[-1]
    lane = lax.broadcasted_iota(jnp.int32, t.shape, 1)
    first_half = (lane & (2 * ROPE_AXIS_FREQS - 1)) < ROPE_AXIS_FREQS
    partner = jnp.where(first_half, pltpu.roll(t, width - ROPE_AXIS_FREQS, 1), pltpu.roll(t, ROPE_AXIS_FREQS, 1))
    return t * cos + partner * sin_signed


def _sub_rows(r):
    return slice(r * SUB_ROWS, (r + 1) * SUB_ROWS)


def _swiglu_hidden(xn_bf16, wgu_ref, act_scr, rows):
    for c in range(N_FF_CHUNKS):
        g = _dot(xn_bf16, wgu_ref[:, c * FF_CHUNK:(c + 1) * FF_CHUNK])
        u = _dot(xn_bf16, wgu_ref[:, D_FF + c * FF_CHUNK:D_FF + (c + 1) * FF_CHUNK])
        act_scr[rows, c * FF_CHUNK:(c + 1) * FF_CHUNK] = (g * jax.nn.sigmoid(g) * u).astype(_BF16)


def _pre_kernel(x_ref, cos_ref, sin_ref, g1_ref, wgu_ref, wdn_ref, gmix_ref, win_ref, gq_ref, gk_ref, ggv_ref,
                wsp_ref, bsp_ref, wbg_ref, eq_ref, ek_ref,
                x1_ref, q_ref, k2_ref, vt_ref, kn_ref, sga_ref, mb_ref, act_scr):
    subs = [_sub_rows(r) for r in range(N_SUB)]
    x = [x_ref[rs, :] for rs in subs]
    xn = [_rms(xi, g1_ref[...]).astype(_BF16) for xi in x]
    for r, rs in enumerate(subs):
        _swiglu_hidden(xn[r], wgu_ref, act_scr, rs)
    x1 = [x[r] + 0.5 * _dot(act_scr[rs, :], wdn_ref[...]) for r, rs in enumerate(subs)]
    for r, rs in enumerate(subs):
        x1_ref[rs, :] = x1[r]
    h = [_rms(xi, gmix_ref[...]).astype(_BF16) for xi in x1]

    o_kv = ATTN_WIDTH
    o_gu = o_kv + 2 * KV_WIDTH
    o_gv = o_gu + GMLP_WIDTH
    o_ga = o_gv + GMLP_WIDTH
    o_gb = o_ga + D_MODEL
    q = [_dot(hr, win_ref[:, 0:o_kv]) for hr in h]
    kv = [_dot(hr, win_ref[:, o_kv:o_gu]) for hr in h]
    u = [_dot(hr, win_ref[:, o_gu:o_gv]) for hr in h]
    vg = [_dot(hr, win_ref[:, o_gv:o_ga]) for hr in h]
    for r, rs in enumerate(subs):
        sga_ref[rs, :] = jax.nn.sigmoid(_dot(h[r], win_ref[:, o_ga:o_gb])).astype(_BF16)
    q_ss = [_group_sumsq(qr, eq_ref[...]) for qr in q]
    k_ss = [_group_sumsq(kvr[:, :KV_WIDTH], ek_ref[...]) for kvr in kv]
    gate_b = [jax.nn.sigmoid(_dot(hr, win_ref[:, o_gb:o_gb + D_MODEL])) for hr in h]

    k = []
    for r, rs in enumerate(subs):
        cos = cos_ref[rs, :]
        sin = sin_ref[rs, :]
        qn = q[r] * lax.rsqrt(q_ss[r] * (1.0 / HEAD_DIM) + EPS) * gq_ref[...]
        qn = _rope(qn, jnp.concatenate([cos] * (ATTN_WIDTH // LANES), axis=1),
                   jnp.concatenate([sin] * (ATTN_WIDTH // LANES), axis=1))
        q_ref[rs, :] = (qn * (HEAD_DIM ** -0.5 * LOG2_E)).astype(_BF16)
        kn = kv[r][:, :KV_WIDTH] * lax.rsqrt(k_ss[r] * (1.0 / HEAD_DIM) + EPS) * gk_ref[...]
        k.append(_rope(kn, cos, sin))

    lane_c = lax.broadcasted_iota(jnp.int32, (CHUNK, LANES), 1)
    sg = []
    for r in range(N_SUB):
        vn = _rms(_gelu(vg[r]), ggv_ref[...])
        mixed_rows = []
        for ci in range(SUB_ROWS // CHUNK):
            cols = []
            for j in range(GMLP_WIDTH // LANES):
                vs = vn[ci * CHUNK:(ci + 1) * CHUNK, j * LANES:(j + 1) * LANES].astype(_BF16)
                zero = jnp.zeros_like(vs)
                rhs = jnp.concatenate([jnp.where(lane_c < GMLP_GROUP_DIM, vs, zero),
                                       jnp.where(lane_c < GMLP_GROUP_DIM, zero, vs)], axis=0)
                cols.append(_dot(wsp_ref[j], rhs))
            mixed_rows.append(jnp.concatenate(cols, axis=1) + bsp_ref[...])
        sg.append((_gelu(u[r]) * jnp.concatenate(mixed_rows, axis=0)).astype(_BF16))

    k_norm2 = [jnp.max(_group_sumsq(kr, ek_ref[...]), axis=0, keepdims=True) for kr in k]
    kn_ref[0] = jnp.broadcast_to(functools.reduce(jnp.maximum, k_norm2), kn_ref.shape[1:])
    for r, rs in enumerate(subs):
        mb_ref[rs, :] = (gate_b[r] * _dot(sg[r], wbg_ref[...])).astype(_BF16)

    ones = jnp.ones((V_ROWS - HEAD_DIM, SUB_ROWS), _BF16)
    lane = lax.broadcasted_iota(jnp.int32, (SUB_ROWS, LANES), 1)
    for r, rs in enumerate(subs):
        k_swapped = pltpu.roll(k[r], HEAD_DIM, 1)
        k2_ref[0, rs, :] = jnp.where(lane < HEAD_DIM, k[r], k_swapped).astype(_BF16)
        k2_ref[1, rs, :] = jnp.where(lane < HEAD_DIM, k_swapped, k[r]).astype(_BF16)
        vt = kv[r][:, KV_WIDTH:].T.astype(_BF16)
        t, lanes = divmod(r * SUB_ROWS, KEY_TILE)
        for kh in range(N_KV_HEADS):
            vt_ref[kh, t, 0:HEAD_DIM, lanes:lanes + SUB_ROWS] = vt[kh * HEAD_DIM:(kh + 1) * HEAD_DIM, :]
            vt_ref[kh, t, HEAD_DIM:V_ROWS, lanes:lanes + SUB_ROWS] = ones


def _attn_kernel(q_ref, k_ref, vt_ref, kn_ref, o_ref, qs_scr, c_scr, acc_scr, p_scr, *, n_key_tiles):
    tq = q_ref.shape[0]
    lane = lax.broadcasted_iota(jnp.int32, (tq, LANES), 1)
    for hq in range(Q_PER_KV):
        pair = q_ref[:, (hq // 2) * LANES:(hq // 2 + 1) * LANES]
        keep = (lane < HEAD_DIM) if hq % 2 == 0 else (lane >= HEAD_DIM)
        qs_scr[hq * tq:(hq + 1) * tq, :] = jnp.where(keep, pair, jnp.zeros_like(pair))
    acc_scr[...] = jnp.zeros(acc_scr.shape, _F32)

    def scores(kt, c):
        start = pl.multiple_of(kt * KEY_TILE, KEY_TILE)
        return lax.dot_general(k_ref[pl.ds(start, KEY_TILE), :], qs_scr[c * tq:(c + 1) * tq, :],
                               (((1,), (1,)), ((), ())), preferred_element_type=_F32)

    qf = qs_scr[...].astype(_F32)
    q_norm2 = lax.dot_general(jnp.ones((8, LANES), _BF16), (qf * qf).astype(_BF16), (((1,), (1,)), ((), ())),
                              preferred_element_type=_F32)
    k_norm2 = jnp.max(kn_ref[...], axis=0)
    lane8 = lax.broadcasted_iota(jnp.int32, k_norm2.shape, 1)
    head_lo = pl.program_id(1) * HEAD_DIM
    this_head = (lane8 >= head_lo) & (lane8 < head_lo + HEAD_DIM)
    k_max2 = jnp.max(jnp.where(this_head, k_norm2, 0.0), axis=1, keepdims=True)
    bound = jnp.sqrt(q_norm2 * k_max2) * SHIFT_MARGIN
    c_scr[...] = bound[0:1, :]

    def probabilities(kt, slot):
        for c in range(Q_PER_KV):
            cols = slice(c * tq, (c + 1) * tq)
            p_scr[slot, :, cols] = jnp.exp2(scores(kt, c) - c_scr[:, cols]).astype(_BF16)

    def accumulate(kt, slot):
        for c in range(Q_PER_KV):
            cols = slice(c * tq, (c + 1) * tq)
            acc_scr[:, cols] += _dot(vt_ref[kt], p_scr[slot, :, cols])

    probabilities(0, 0)

    @pl.when(jnp.max(bound) > SAFE_SHIFT_MAX)
    def _():
        def running_max(kt, m):
            tile_max = [jnp.max(scores(kt, c), axis=0, keepdims=True) for c in range(Q_PER_KV)]
            return jnp.maximum(m, jnp.concatenate(tile_max, axis=1))
        c_scr[...] = lax.fori_loop(0, n_key_tiles, running_max, jnp.full(c_scr.shape, NEG_BIG, _F32))
        probabilities(0, 0)

    unroll = min(KEY_TILES_PER_ITER, n_key_tiles // 2)

    def key_tile_group(i, carry):
        for j in range(unroll):
            kt = unroll * i + j
            probabilities(kt + 1 if j + 1 < unroll else jnp.minimum(kt + 1, n_key_tiles - 1), (j + 1) % 2)
            accumulate(kt, j % 2)
        return carry

    assert n_key_tiles % unroll == 0 and unroll % 2 == 0
    lax.fori_loop(0, n_key_tiles // unroll, key_tile_group, 0)

    acc = acc_scr[...]
    out_t = acc[0:HEAD_DIM, :] / acc[HEAD_DIM:HEAD_DIM + 1, :]
    for j in range(Q_PER_KV // 2):
        pair_t = jnp.concatenate([out_t[:, (2 * j) * tq:(2 * j + 1) * tq],
                                  out_t[:, (2 * j + 1) * tq:(2 * j + 2) * tq]], axis=0)
        o_ref[:, j * LANES:(j + 1) * LANES] = pair_t.T.astype(_BF16)


def _post_kernel(a_ref, sga_ref, mb_ref, x1_ref, p_ref, wba_ref, wout_ref, g2_ref, wgu_ref, wdn_ref, gple_ref,
                 wpg_ref, wple_ref, gfin_ref, y_ref, act_scr):
    subs = [_sub_rows(r) for r in range(N_SUB)]
    branch = [_dot(a_ref[rs, :], wba_ref[...]) for rs in subs]
    ple = [_dot(p_ref[rs, :].astype(_BF16), wple_ref[...]) for rs in subs]
    merged = [(sga_ref[rs, :].astype(_F32) * branch[r] + mb_ref[rs, :].astype(_F32)).astype(_BF16)
              for r, rs in enumerate(subs)]
    x2 = [x1_ref[rs, :] + _dot(merged[r], wout_ref[...]) for r, rs in enumerate(subs)]
    xn = [_rms(xi, g2_ref[...]).astype(_BF16) for xi in x2]
    for r, rs in enumerate(subs):
        _swiglu_hidden(xn[r], wgu_ref, act_scr, rs)
    x3 = [x2[r] + 0.5 * _dot(act_scr[rs, :], wdn_ref[...]) for r, rs in enumerate(subs)]
    gate = [jax.nn.sigmoid(_dot(_rms(xi, gple_ref[...]).astype(_BF16), wpg_ref[...])) for xi in x3]
    for r, rs in enumerate(subs):
        y_ref[rs, :] = _rms(x3[r] + gate[r] * ple[r], gfin_ref[...])


def _resident(shape):
    nd = len(shape)
    return pl.BlockSpec(shape, lambda *_: (0,) * nd, pipeline_mode=pl.Buffered(1))


def _rows(width, tile=ROW_TILE):
    return pl.BlockSpec((tile, width), lambda i: (i, 0))


def _rope_tables(seq):
    rows = seq // GRID_W
    row = jnp.repeat(jnp.arange(rows, dtype=_F32), GRID_W)
    col = jnp.tile(jnp.arange(GRID_W, dtype=_F32), rows)
    inv = jnp.power(jnp.float32(ROPE_THETA), -jnp.arange(ROPE_AXIS_FREQS, dtype=_F32) / ROPE_AXIS_FREQS)
    ang = jnp.stack([row[:, None] * inv, col[:, None] * inv], axis=1)
    cos = jnp.cos(ang)
    sin = jnp.sin(ang)
    cos_h = jnp.concatenate([cos[:, 0], cos[:, 0], cos[:, 1], cos[:, 1]], axis=1)
    sin_h = jnp.concatenate([-sin[:, 0], sin[:, 0], -sin[:, 1], sin[:, 1]], axis=1)
    return jnp.tile(cos_h, (1, LANES // HEAD_DIM)), jnp.tile(sin_h, (1, LANES // HEAD_DIM))


def _block_diag_ones(width):
    idx = jnp.arange(width) // HEAD_DIM
    return (idx[:, None] == idx[None, :]).astype(_BF16)


def _trunk(x, p, prm, cos_t, sin_t):
    batch, seq, _ = x.shape
    rows = batch * seq
    n_tiles = rows // ROW_TILE
    seq_tiles = seq // ROW_TILE
    x2d = x.reshape(rows, D_MODEL)
    p2d = p.reshape(rows, PLE_DIM)
    cparams = pltpu.CompilerParams(dimension_semantics=("arbitrary",), vmem_limit_bytes=VMEM_LIMIT)

    pos_spec = pl.BlockSpec((ROW_TILE, LANES), lambda i: (i % seq_tiles, 0))
    x1, q, k2, vt, kn, sga, mb = pl.pallas_call(
        _pre_kernel,
        grid=(n_tiles,),
        in_specs=[_rows(D_MODEL), pos_spec, pos_spec,
                  _resident((1, D_MODEL)), _resident((D_MODEL, 2 * D_FF)), _resident((D_FF, D_MODEL)),
                  _resident((1, D_MODEL)), _resident(prm["w_in"].shape),
                  _resident((1, ATTN_WIDTH)), _resident((1, KV_WIDTH)), _resident((1, GMLP_WIDTH)),
                  _resident(prm["w_sp"].shape), _resident((CHUNK, GMLP_WIDTH)), _resident((GMLP_WIDTH, D_MODEL)),
                  _resident((ATTN_WIDTH, ATTN_WIDTH)), _resident((KV_WIDTH, KV_WIDTH))],
        out_specs=[_rows(D_MODEL), _rows(ATTN_WIDTH),
                   pl.BlockSpec((N_KV_HEADS, ROW_TILE, LANES), lambda i: (0, i, 0)),
                   pl.BlockSpec((N_KV_HEADS, ROW_TILE // KEY_TILE, V_ROWS, KEY_TILE), lambda i: (0, i, 0, 0)),
                   pl.BlockSpec((1, 8, LANES), lambda i: (i, 0, 0)),
                   _rows(D_MODEL), _rows(D_MODEL)],
        out_shape=[jax.ShapeDtypeStruct((rows, D_MODEL), _F32),
                   jax.ShapeDtypeStruct((rows, ATTN_WIDTH), _BF16),
                   jax.ShapeDtypeStruct((N_KV_HEADS, rows, LANES), _BF16),
                   jax.ShapeDtypeStruct((N_KV_HEADS, rows // KEY_TILE, V_ROWS, KEY_TILE), _BF16),
                   jax.ShapeDtypeStruct((n_tiles, 8, LANES), _F32),
                   jax.ShapeDtypeStruct((rows, D_MODEL), _BF16),
                   jax.ShapeDtypeStruct((rows, D_MODEL), _BF16)],
        scratch_shapes=[pltpu.VMEM((ROW_TILE, D_FF), _BF16)],
        compiler_params=cparams,
        name="pre",
    )(x2d, cos_t, sin_t, prm["g_ffn1"], prm["w_gu1"], prm["w_dn1"], prm["g_mix"], prm["w_in"],
      prm["g_q"], prm["g_k"], prm["g_gv"], prm["w_sp"], prm["b_sp"], prm["w_bg"], prm["e_q"], prm["e_k"])

    q_tiles = seq // Q_TILE
    a = pl.pallas_call(
        functools.partial(_attn_kernel, n_key_tiles=seq // KEY_TILE),
        grid=(batch, N_KV_HEADS, q_tiles),
        in_specs=[pl.BlockSpec((Q_TILE, 2 * LANES), lambda b, kh, qi: (b * q_tiles + qi, kh)),
                  pl.BlockSpec((None, seq, LANES), lambda b, kh, qi: (kh, b, 0)),
                  pl.BlockSpec((None, seq // KEY_TILE, V_ROWS, KEY_TILE), lambda b, kh, qi: (kh, b, 0, 0)),
                  pl.BlockSpec((seq_tiles, 8, LANES), lambda b, kh, qi: (b, 0, 0))],
        out_specs=pl.BlockSpec((Q_TILE, 2 * LANES), lambda b, kh, qi: (b * q_tiles + qi, kh)),
        out_shape=jax.ShapeDtypeStruct((rows, ATTN_WIDTH), _BF16),
        scratch_shapes=[pltpu.VMEM((Q_PER_KV * Q_TILE, LANES), _BF16),
                        pltpu.VMEM((1, Q_PER_KV * Q_TILE), _F32),
                        pltpu.VMEM((V_ROWS, Q_PER_KV * Q_TILE), _F32),
                        pltpu.VMEM((2, KEY_TILE, Q_PER_KV * Q_TILE), _BF16)],
        compiler_params=pltpu.CompilerParams(dimension_semantics=("arbitrary",) * 3, vmem_limit_bytes=VMEM_LIMIT),
        name="attn",
    )(q, k2, vt, kn)

    y = pl.pallas_call(
        _post_kernel,
        grid=(n_tiles,),
        in_specs=[_rows(ATTN_WIDTH), _rows(D_MODEL), _rows(D_MODEL), _rows(D_MODEL), _rows(PLE_DIM),
                  _resident((ATTN_WIDTH, D_MODEL)), _resident((D_MODEL, D_MODEL)),
                  _resident((1, D_MODEL)), _resident((D_MODEL, 2 * D_FF)), _resident((D_FF, D_MODEL)),
                  _resident((1, D_MODEL)), _resident((D_MODEL, D_MODEL)), _resident((PLE_DIM, D_MODEL)),
                  _resident((1, D_MODEL))],
        out_specs=_rows(D_MODEL),
        out_shape=jax.ShapeDtypeStruct((rows, D_MODEL), _F32),
        scratch_shapes=[pltpu.VMEM((ROW_TILE, D_FF), _BF16)],
        compiler_params=cparams,
        name="post",
    )(a, sga, mb, x1, p2d, prm["w_ba"], prm["w_out"], prm["g_ffn2"], prm["w_gu2"], prm["w_dn2"],
      prm["g_ple"], prm["w_pg"], prm["w_ple"], prm["g_final"])
    return y.reshape(batch, seq, D_MODEL)


def kernel(x_prompt, x_sample, p_prompt, p_sample, g_ffn1, w_ffn1_gu, w_ffn1_down, g_mix, w_in, g_q, g_k, g_gmlp_v, w_spatial, b_spatial, w_branch_attn, w_branch_gmlp, w_out, g_ffn2, w_ffn2_gu, w_ffn2_down, g_ple, w_ple_gate, w_ple, g_final):
    assert g_ffn1.shape[0] == 1, "the post kernel fuses the final norm into the single layer"
    cos_t, sin_t = _rope_tables(max(x_prompt.shape[1], x_sample.shape[1]))
    w_sp = w_spatial[0].reshape(GMLP_GROUPS // 2, 2, CHUNK, CHUNK).transpose(0, 2, 1, 3)
    prm = dict(
        g_ffn1=g_ffn1[0][None], w_gu1=w_ffn1_gu[0].astype(_BF16), w_dn1=w_ffn1_down[0].astype(_BF16),
        g_mix=g_mix[0][None], w_in=w_in[0].astype(_BF16),
        g_q=jnp.tile(g_q[0], N_HEADS)[None], g_k=jnp.tile(g_k[0], N_KV_HEADS)[None],
        g_gv=g_gmlp_v[0][None],
        w_sp=w_sp.reshape(GMLP_GROUPS // 2, CHUNK, 2 * CHUNK).astype(_BF16),
        b_sp=jnp.repeat(b_spatial[0].T, GMLP_GROUP_DIM, axis=1),
        w_bg=w_branch_gmlp[0].astype(_BF16), w_ba=w_branch_attn[0].astype(_BF16),
        w_out=w_out[0].astype(_BF16), g_ffn2=g_ffn2[0][None], w_gu2=w_ffn2_gu[0].astype(_BF16),
        w_dn2=w_ffn2_down[0].astype(_BF16), g_ple=g_ple[0][None], w_pg=w_ple_gate[0].astype(_BF16),
        w_ple=w_ple[0].astype(_BF16), g_final=g_final[None],
        e_q=_block_diag_ones(ATTN_WIDTH), e_k=_block_diag_ones(KV_WIDTH),
    )
    return (_trunk(x_prompt, p_prompt[0], prm, cos_t, sin_t), _trunk(x_sample, p_sample[0], prm, cos_t, sin_t))
```

```python
import functools

import jax
import jax.numpy as jnp
from jax import lax
from jax.experimental import pallas as pl
from jax.experimental.pallas import tpu as pltpu

D_MODEL = 1024
N_HEADS = 8
N_KV_HEADS = 2
HEAD_DIM = 64
Q_PER_KV = N_HEADS // N_KV_HEADS
ATTN_WIDTH = N_HEADS * HEAD_DIM
KV_WIDTH = N_KV_HEADS * HEAD_DIM
GMLP_GROUPS = 8
GMLP_GROUP_DIM = 64
GMLP_WIDTH = GMLP_GROUPS * GMLP_GROUP_DIM
CHUNK = 128
GRID_W = 64
ROPE_THETA = 10000.0
ROPE_AXIS_FREQS = HEAD_DIM // 4
D_FF = 2816
PLE_DIM = 256
EPS = 1e-6

LANES = 128
ROW_TILE = 512
SUB_ROWS = 256
N_SUB = ROW_TILE // SUB_ROWS
KEY_TILE = 512
Q_TILE = 256
Q_SUBTILES = 2
KEY_TILES_PER_ITER = 16
FF_CHUNK = 256
N_FF_CHUNKS = D_FF // FF_CHUNK
V_ROWS = 2 * HEAD_DIM
NEG_BIG = -1e30
LOG2_E = 1.4426950408889634
SAFE_SHIFT_MAX = 50.0
SHIFT_MARGIN = 1.01
VMEM_LIMIT = 56 * 1024 * 1024

_F32 = jnp.float32
_BF16 = jnp.bfloat16


def _dot(a, b):
    return jnp.dot(a, b, preferred_element_type=_F32)


def _rms(x, g):
    return x * lax.rsqrt(jnp.mean(x * x, axis=-1, keepdims=True) + EPS) * g


def _gelu(x):
    return 0.5 * x * (1.0 + lax.erf(x * (2.0 ** -0.5)))


def _group_sumsq(t, ones_blockdiag):
    sq = t * t
    hi = sq.astype(_BF16)
    lo = (sq - hi.astype(_F32)).astype(_BF16)
    return _dot(hi, ones_blockdiag) + _dot(lo, ones_blockdiag)


def _rope(t, cos, sin_signed):
    width = t.shape[-1]
    lane = lax.broadcasted_iota(jnp.int32, t.shape, 1)
    first_half = (lane & (2 * ROPE_AXIS_FREQS - 1)) < ROPE_AXIS_FREQS
    partner = jnp.where(first_half, pltpu.roll(t, width - ROPE_AXIS_FREQS, 1), pltpu.roll(t, ROPE_AXIS_FREQS, 1))
    return t * cos + partner * sin_signed


def _sub_rows(r):
    return slice(r * SUB_ROWS, (r + 1) * SUB_ROWS)


def _swiglu_hidden(xn_bf16, wgu_ref, act_scr, rows):
    for c in range(N_FF_CHUNKS):
        g = _dot(xn_bf16, wgu_ref[:, c * FF_CHUNK:(c + 1) * FF_CHUNK])
        u = _dot(xn_bf16, wgu_ref[:, D_FF + c * FF_CHUNK:D_FF + (c + 1) * FF_CHUNK])
        act_scr[rows, c * FF_CHUNK:(c + 1) * FF_CHUNK] = (g * jax.nn.sigmoid(g) * u).astype(_BF16)


def _pre_kernel(x_ref, cos_ref, sin_ref, g1_ref, wgu_ref, wdn_ref, gmix_ref, win_ref, gq_ref, gk_ref, ggv_ref,
                wsp_ref, bsp_ref, wbg_ref, eq_ref, ek_ref,
                x1_ref, q_ref, k2_ref, vt_ref, kn_ref, sga_ref, mb_ref, act_scr):
    subs = [_sub_rows(r) for r in range(N_SUB)]
    x = [x_ref[rs, :] for rs in subs]
    xn = [_rms(xi, g1_ref[...]).astype(_BF16) for xi in x]
    for r, rs in enumerate(subs):
        _swiglu_hidden(xn[r], wgu_ref, act_scr, rs)
    x1 = [x[r] + 0.5 * _dot(act_scr[rs, :], wdn_ref[...]) for r, rs in enumerate(subs)]
    for r, rs in enumerate(subs):
        x1_ref[rs, :] = x1[r]
    h = [_rms(xi, gmix_ref[...]).astype(_BF16) for xi in x1]

    o_kv = ATTN_WIDTH
    o_gu = o_kv + 2 * KV_WIDTH
    o_gv = o_gu + GMLP_WIDTH
    o_ga = o_gv + GMLP_WIDTH
    o_gb = o_ga + D_MODEL
    q = [_dot(hr, win_ref[:, 0:o_kv]) for hr in h]
    kv = [_dot(hr, win_ref[:, o_kv:o_gu]) for hr in h]
    u = [_dot(hr, win_ref[:, o_gu:o_gv]) for hr in h]
    vg = [_dot(hr, win_ref[:, o_gv:o_ga]) for hr in h]
    for r, rs in enumerate(subs):
        sga_ref[rs, :] = jax.nn.sigmoid(_dot(h[r], win_ref[:, o_ga:o_gb])).astype(_BF16)
    q_ss = [_group_sumsq(qr, eq_ref[...]) for qr in q]
    k_ss = [_group_sumsq(kvr[:, :KV_WIDTH], ek_ref[...]) for kvr in kv]
    gate_b = [jax.nn.sigmoid(_dot(hr, win_ref[:, o_gb:o_gb + D_MODEL])) for hr in h]

    k = []
    for r, rs in enumerate(subs):
        cos = cos_ref[rs, :]
        sin = sin_ref[rs, :]
        qn = q[r] * lax.rsqrt(q_ss[r] * (1.0 / HEAD_DIM) + EPS) * gq_ref[...]
        qn = _rope(qn, jnp.concatenate([cos] * (ATTN_WIDTH // LANES), axis=1),
                   jnp.concatenate([sin] * (ATTN_WIDTH // LANES), axis=1))
        q_ref[rs, :] = (qn * (HEAD_DIM ** -0.5 * LOG2_E)).astype(_BF16)
        kn = kv[r][:, :KV_WIDTH] * lax.rsqrt(k_ss[r] * (1.0 / HEAD_DIM) + EPS) * gk_ref[...]
        k.append(_rope(kn, cos, sin))

    lane_c = lax.broadcasted_iota(jnp.int32, (CHUNK, LANES), 1)
    sg = []
    for r in range(N_SUB):
        vn = _rms(_gelu(vg[r]), ggv_ref[...])
        mixed_rows = []
        for ci in range(SUB_ROWS // CHUNK):
            cols = []
            for j in range(GMLP_WIDTH // LANES):
                vs = vn[ci * CHUNK:(ci + 1) * CHUNK, j * LANES:(j + 1) * LANES].astype(_BF16)
                zero = jnp.zeros_like(vs)
                rhs = jnp.concatenate([jnp.where(lane_c < GMLP_GROUP_DIM, vs, zero),
                                       jnp.where(lane_c < GMLP_GROUP_DIM, zero, vs)], axis=0)
                cols.append(_dot(wsp_ref[j], rhs))
            mixed_rows.append(jnp.concatenate(cols, axis=1) + bsp_ref[...])
        sg.append((_gelu(u[r]) * jnp.concatenate(mixed_rows, axis=0)).astype(_BF16))

    k_norm2 = [jnp.max(_group_sumsq(kr, ek_ref[...]), axis=0, keepdims=True) for kr in k]
    kn_ref[0] = jnp.broadcast_to(functools.reduce(jnp.maximum, k_norm2), kn_ref.shape[1:])
    for r, rs in enumerate(subs):
        mb_ref[rs, :] = (gate_b[r] * _dot(sg[r], wbg_ref[...])).astype(_BF16)

    ones = jnp.ones((V_ROWS - HEAD_DIM, SUB_ROWS), _BF16)
    lane = lax.broadcasted_iota(jnp.int32, (SUB_ROWS, LANES), 1)
    for r, rs in enumerate(subs):
        k_swapped = pltpu.roll(k[r], HEAD_DIM, 1)
        k2_ref[0, rs, :] = jnp.where(lane < HEAD_DIM, k[r], k_swapped).astype(_BF16)
        k2_ref[1, rs, :] = jnp.where(lane < HEAD_DIM, k_swapped, k[r]).astype(_BF16)
        vt = kv[r][:, KV_WIDTH:].T.astype(_BF16)
        t, lanes = divmod(r * SUB_ROWS, KEY_TILE)
        for kh in range(N_KV_HEADS):
            vt_ref[kh, t, 0:HEAD_DIM, lanes:lanes + SUB_ROWS] = vt[kh * HEAD_DIM:(kh + 1) * HEAD_DIM, :]
            vt_ref[kh, t, HEAD_DIM:V_ROWS, lanes:lanes + SUB_ROWS] = ones


def _attn_kernel(q_ref, k_ref, vt_ref, kn_ref, o_ref, qs_scr, c_scr, acc_scr, p_scr, *, n_key_tiles):
    tq = Q_TILE
    lane = lax.broadcasted_iota(jnp.int32, (tq, LANES), 1)
    for u in range(Q_SUBTILES):
        for hq in range(Q_PER_KV):
            pair = q_ref[u * tq:(u + 1) * tq, (hq // 2) * LANES:(hq // 2 + 1) * LANES]
            keep = (lane < HEAD_DIM) if hq % 2 == 0 else (lane >= HEAD_DIM)
            qs_scr[u, hq * tq:(hq + 1) * tq, :] = jnp.where(keep, pair, jnp.zeros_like(pair))
    acc_scr[...] = jnp.zeros(acc_scr.shape, _F32)

    def scores(u, kt, c):
        start = pl.multiple_of(kt * KEY_TILE, KEY_TILE)
        return lax.dot_general(k_ref[pl.ds(start, KEY_TILE), :], qs_scr[u, c * tq:(c + 1) * tq, :],
                               (((1,), (1,)), ((), ())), preferred_element_type=_F32)

    k_norm2 = jnp.max(kn_ref[...], axis=0)
    lane8 = lax.broadcasted_iota(jnp.int32, k_norm2.shape, 1)
    head_lo = pl.program_id(1) * HEAD_DIM
    this_head = (lane8 >= head_lo) & (lane8 < head_lo + HEAD_DIM)
    k_max2 = jnp.max(jnp.where(this_head, k_norm2, 0.0), axis=1, keepdims=True)
    largest_bound = None
    for u in range(Q_SUBTILES):
        qf = qs_scr[u].astype(_F32)
        q_norm2 = lax.dot_general(jnp.ones((8, LANES), _BF16), (qf * qf).astype(_BF16), (((1,), (1,)), ((), ())),
                                  preferred_element_type=_F32)
        bound = jnp.sqrt(q_norm2 * k_max2) * SHIFT_MARGIN
        c_scr[u] = bound[0:1, :]
        largest_bound = jnp.max(bound) if largest_bound is None else jnp.maximum(largest_bound, jnp.max(bound))

    def probabilities(u, kt, slot):
        for c in range(Q_PER_KV):
            cols = slice(c * tq, (c + 1) * tq)
            p_scr[slot, :, cols] = jnp.exp2(scores(u, kt, c) - c_scr[u, :, cols]).astype(_BF16)

    def accumulate(u, kt, slot):
        for c in range(Q_PER_KV):
            cols = slice(c * tq, (c + 1) * tq)
            acc_scr[u, :, cols] += _dot(vt_ref[kt], p_scr[slot, :, cols])

    probabilities(0, 0, 0)

    @pl.when(largest_bound > SAFE_SHIFT_MAX)
    def _():
        for u in range(Q_SUBTILES):
            def running_max(kt, m, u=u):
                tile_max = [jnp.max(scores(u, kt, c), axis=0, keepdims=True) for c in range(Q_PER_KV)]
                return jnp.maximum(m, jnp.concatenate(tile_max, axis=1))
            c_scr[u] = lax.fori_loop(0, n_key_tiles, running_max, jnp.full(c_scr.shape[1:], NEG_BIG, _F32))
        probabilities(0, 0, 0)

    unroll = min(KEY_TILES_PER_ITER, n_key_tiles // 2)
    assert n_key_tiles % unroll == 0 and unroll % 2 == 0
    groups_per_query_tile = n_key_tiles // unroll
    n_stages = Q_SUBTILES * n_key_tiles

    def key_tile_group(g, carry):
        u = lax.div(g, groups_per_query_tile)
        kt0 = lax.rem(g, groups_per_query_tile) * unroll
        for j in range(unroll):
            if j + 1 < unroll:
                u_next, kt_next = u, kt0 + j + 1
            else:
                stage = jnp.minimum((g + 1) * unroll, n_stages - 1)
                u_next, kt_next = lax.div(stage, n_key_tiles), lax.rem(stage, n_key_tiles)
            probabilities(u_next, kt_next, (j + 1) % 2)
            accumulate(u, kt0 + j, j % 2)
        return carry

    lax.fori_loop(0, Q_SUBTILES * groups_per_query_tile, key_tile_group, 0)

    for u in range(Q_SUBTILES):
        acc = acc_scr[u]
        out_t = acc[0:HEAD_DIM, :] / acc[HEAD_DIM:HEAD_DIM + 1, :]
        for j in range(Q_PER_KV // 2):
            pair_t = jnp.concatenate([out_t[:, (2 * j) * tq:(2 * j + 1) * tq],
                                      out_t[:, (2 * j + 1) * tq:(2 * j + 2) * tq]], axis=0)
            o_ref[u * tq:(u + 1) * tq, j * LANES:(j + 1) * LANES] = pair_t.T.astype(_BF16)


def _post_kernel(a_ref, sga_ref, mb_ref, x1_ref, p_ref, wba_ref, wout_ref, g2_ref, wgu_ref, wdn_ref, gple_ref,
                 wpg_ref, wple_ref, gfin_ref, y_ref, act_scr):
    subs = [_sub_rows(r) for r in range(N_SUB)]
    branch = [_dot(a_ref[rs, :], wba_ref[...]) for rs in subs]
    ple = [_dot(p_ref[rs, :].astype(_BF16), wple_ref[...]) for rs in subs]
    merged = [(sga_ref[rs, :].astype(_F32) * branch[r] + mb_ref[rs, :].astype(_F32)).astype(_BF16)
              for r, rs in enumerate(subs)]
    x2 = [x1_ref[rs, :] + _dot(merged[r], wout_ref[...]) for r, rs in enumerate(subs)]
    xn = [_rms(xi, g2_ref[...]).astype(_BF16) for xi in x2]
    for r, rs in enumerate(subs):
        _swiglu_hidden(xn[r], wgu_ref, act_scr, rs)
    x3 = [x2[r] + 0.5 * _dot(act_scr[rs, :], wdn_ref[...]) for r, rs in enumerate(subs)]
    gate = [jax.nn.sigmoid(_dot(_rms(xi, gple_ref[...]).astype(_BF16), wpg_ref[...])) for xi in x3]
    for r, rs in enumerate(subs):
        y_ref[rs, :] = _rms(x3[r] + gate[r] * ple[r], gfin_ref[...])


def _resident(shape):
    nd = len(shape)
    return pl.BlockSpec(shape, lambda *_: (0,) * nd, pipeline_mode=pl.Buffered(1))


def _rows(width, tile=ROW_TILE):
    return pl.BlockSpec((tile, width), lambda i: (i, 0))


def _rope_tables(seq):
    rows = seq // GRID_W
    row = jnp.repeat(jnp.arange(rows, dtype=_F32), GRID_W)
    col = jnp.tile(jnp.arange(GRID_W, dtype=_F32), rows)
    inv = jnp.power(jnp.float32(ROPE_THETA), -jnp.arange(ROPE_AXIS_FREQS, dtype=_F32) / ROPE_AXIS_FREQS)
    ang = jnp.stack([row[:, None] * inv, col[:, None] * inv], axis=1)
    cos = jnp.cos(ang)
    sin = jnp.sin(ang)
    cos_h = jnp.concatenate([cos[:, 0], cos[:, 0], cos[:, 1], cos[:, 1]], axis=1)
    sin_h = jnp.concatenate([-sin[:, 0], sin[:, 0], -sin[:, 1], sin[:, 1]], axis=1)
    return jnp.tile(cos_h, (1, LANES // HEAD_DIM)), jnp.tile(sin_h, (1, LANES // HEAD_DIM))


def _block_diag_ones(width):
    idx = jnp.arange(width) // HEAD_DIM
    return (idx[:, None] == idx[None, :]).astype(_BF16)


def _trunk(x, p, prm, cos_t, sin_t):
    batch, seq, _ = x.shape
    rows = batch * seq
    n_tiles = rows // ROW_TILE
    seq_tiles = seq // ROW_TILE
    x2d = x.reshape(rows, D_MODEL)
    p2d = p.reshape(rows, PLE_DIM)
    cparams = pltpu.CompilerParams(dimension_semantics=("arbitrary",), vmem_limit_bytes=VMEM_LIMIT)

    pos_spec = pl.BlockSpec((ROW_TILE, LANES), lambda i: (i % seq_tiles, 0))
    x1, q, k2, vt, kn, sga, mb = pl.pallas_call(
        _pre_kernel,
        grid=(n_tiles,),
        in_specs=[_rows(D_MODEL), pos_spec, pos_spec,
                  _resident((1, D_MODEL)), _resident((D_MODEL, 2 * D_FF)), _resident((D_FF, D_MODEL)),
                  _resident((1, D_MODEL)), _resident(prm["w_in"].shape),
                  _resident((1, ATTN_WIDTH)), _resident((1, KV_WIDTH)), _resident((1, GMLP_WIDTH)),
                  _resident(prm["w_sp"].shape), _resident((CHUNK, GMLP_WIDTH)), _resident((GMLP_WIDTH, D_MODEL)),
                  _resident((ATTN_WIDTH, ATTN_WIDTH)), _resident((KV_WIDTH, KV_WIDTH))],
        out_specs=[_rows(D_MODEL), _rows(ATTN_WIDTH),
                   pl.BlockSpec((N_KV_HEADS, ROW_TILE, LANES), lambda i: (0, i, 0)),
                   pl.BlockSpec((N_KV_HEADS, ROW_TILE // KEY_TILE, V_ROWS, KEY_TILE), lambda i: (0, i, 0, 0)),
                   pl.BlockSpec((1, 8, LANES), lambda i: (i, 0, 0)),
                   _rows(D_MODEL), _rows(D_MODEL)],
        out_shape=[jax.ShapeDtypeStruct((rows, D_MODEL), _F32),
                   jax.ShapeDtypeStruct((rows, ATTN_WIDTH), _BF16),
                   jax.ShapeDtypeStruct((N_KV_HEADS, rows, LANES), _BF16),
                   jax.ShapeDtypeStruct((N_KV_HEADS, rows // KEY_TILE, V_ROWS, KEY_TILE), _BF16),
                   jax.ShapeDtypeStruct((n_tiles, 8, LANES), _F32),
                   jax.ShapeDtypeStruct((rows, D_MODEL), _BF16),
                   jax.ShapeDtypeStruct((rows, D_MODEL), _BF16)],
        scratch_shapes=[pltpu.VMEM((ROW_TILE, D_FF), _BF16)],
        compiler_params=cparams,
        name="pre",
    )(x2d, cos_t, sin_t, prm["g_ffn1"], prm["w_gu1"], prm["w_dn1"], prm["g_mix"], prm["w_in"],
      prm["g_q"], prm["g_k"], prm["g_gv"], prm["w_sp"], prm["b_sp"], prm["w_bg"], prm["e_q"], prm["e_k"])

    q_tiles = seq // (Q_SUBTILES * Q_TILE)
    a = pl.pallas_call(
        functools.partial(_attn_kernel, n_key_tiles=seq // KEY_TILE),
        grid=(batch, N_KV_HEADS, q_tiles),
        in_specs=[pl.BlockSpec((Q_SUBTILES * Q_TILE, 2 * LANES), lambda b, kh, qi: (b * q_tiles + qi, kh)),
                  pl.BlockSpec((None, seq, LANES), lambda b, kh, qi: (kh, b, 0)),
                  pl.BlockSpec((None, seq // KEY_TILE, V_ROWS, KEY_TILE), lambda b, kh, qi: (kh, b, 0, 0)),
                  pl.BlockSpec((seq_tiles, 8, LANES), lambda b, kh, qi: (b, 0, 0))],
        out_specs=pl.BlockSpec((Q_SUBTILES * Q_TILE, 2 * LANES), lambda b, kh, qi: (b * q_tiles + qi, kh)),
        out_shape=jax.ShapeDtypeStruct((rows, ATTN_WIDTH), _BF16),
        scratch_shapes=[pltpu.VMEM((Q_SUBTILES, Q_PER_KV * Q_TILE, LANES), _BF16),
                        pltpu.VMEM((Q_SUBTILES, 1, Q_PER_KV * Q_TILE), _F32),
                        pltpu.VMEM((Q_SUBTILES, V_ROWS, Q_PER_KV * Q_TILE), _F32),
                        pltpu.VMEM((2, KEY_TILE, Q_PER_KV * Q_TILE), _BF16)],
        compiler_params=pltpu.CompilerParams(dimension_semantics=("arbitrary",) * 3, vmem_limit_bytes=VMEM_LIMIT),
        name="attn",
    )(q, k2, vt, kn)

    y = pl.pallas_call(
        _post_kernel,
        grid=(n_tiles,),
        in_specs=[_rows(ATTN_WIDTH), _rows(D_MODEL), _rows(D_MODEL), _rows(D_MODEL), _rows(PLE_DIM),
                  _resident((ATTN_WIDTH, D_MODEL)), _resident((D_MODEL, D_MODEL)),
                  _resident((1, D_MODEL)), _resident((D_MODEL, 2 * D_FF)), _resident((D_FF, D_MODEL)),
                  _resident((1, D_MODEL)), _resident((D_MODEL, D_MODEL)), _resident((PLE_DIM, D_MODEL)),
                  _resident((1, D_MODEL))],
        out_specs=_rows(D_MODEL),
        out_shape=jax.ShapeDtypeStruct((rows, D_MODEL), _F32),
        scratch_shapes=[pltpu.VMEM((ROW_TILE, D_FF), _BF16)],
        compiler_params=cparams,
        name="post",
    )(a, sga, mb, x1, p2d, prm["w_ba"], prm["w_out"], prm["g_ffn2"], prm["w_gu2"], prm["w_dn2"],
      prm["g_ple"], prm["w_pg"], prm["w_ple"], prm["g_final"])
    return y.reshape(batch, seq, D_MODEL)


def kernel(x_prompt, x_sample, p_prompt, p_sample, g_ffn1, w_ffn1_gu, w_ffn1_down, g_mix, w_in, g_q, g_k, g_gmlp_v, w_spatial, b_spatial, w_branch_attn, w_branch_gmlp, w_out, g_ffn2, w_ffn2_gu, w_ffn2_down, g_ple, w_ple_gate, w_ple, g_final):
    assert g_ffn1.shape[0] == 1, "the post kernel fuses the final norm into the single layer"
    cos_t, sin_t = _rope_tables(max(x_prompt.shape[1], x_sample.shape[1]))
    w_sp = w_spatial[0].reshape(GMLP_GROUPS // 2, 2, CHUNK, CHUNK).transpose(0, 2, 1, 3)
    prm = dict(
        g_ffn1=g_ffn1[0][None], w_gu1=w_ffn1_gu[0].astype(_BF16), w_dn1=w_ffn1_down[0].astype(_BF16),
        g_mix=g_mix[0][None], w_in=w_in[0].astype(_BF16),
        g_q=jnp.tile(g_q[0], N_HEADS)[None], g_k=jnp.tile(g_k[0], N_KV_HEADS)[None],
        g_gv=g_gmlp_v[0][None],
        w_sp=w_sp.reshape(GMLP_GROUPS // 2, CHUNK, 2 * CHUNK).astype(_BF16),
        b_sp=jnp.repeat(b_spatial[0].T, GMLP_GROUP_DIM, axis=1),
        w_bg=w_branch_gmlp[0].astype(_BF16), w_ba=w_branch_attn[0].astype(_BF16),
        w_out=w_out[0].astype(_BF16), g_ffn2=g_ffn2[0][None], w_gu2=w_ffn2_gu[0].astype(_BF16),
        w_dn2=w_ffn2_down[0].astype(_BF16), g_ple=g_ple[0][None], w_pg=w_ple_gate[0].astype(_BF16),
        w_ple=w_ple[0].astype(_BF16), g_final=g_final[None],
        e_q=_block_diag_ones(ATTN_WIDTH), e_k=_block_diag_ones(KV_WIDTH),
    )
    return (_trunk(x_prompt, p_prompt[0], prm, cos_t, sin_t), _trunk(x_sample, p_sample[0], prm, cos_t, sin_t))
```

```python
import functools

import jax
import jax.numpy as jnp
from jax import lax
from jax.experimental import pallas as pl
from jax.experimental.pallas import tpu as pltpu

D_MODEL = 1024
N_HEADS = 8
N_KV_HEADS = 2
HEAD_DIM = 64
Q_PER_KV = N_HEADS // N_KV_HEADS
ATTN_WIDTH = N_HEADS * HEAD_DIM
KV_WIDTH = N_KV_HEADS * HEAD_DIM
GMLP_GROUPS = 8
GMLP_GROUP_DIM = 64
GMLP_WIDTH = GMLP_GROUPS * GMLP_GROUP_DIM
CHUNK = 128
GRID_W = 64
ROPE_THETA = 10000.0
ROPE_AXIS_FREQS = HEAD_DIM // 4
D_FF = 2816
PLE_DIM = 256
EPS = 1e-6

LANES = 128
ROW_TILE = 512
SUB_ROWS = 256
N_SUB = ROW_TILE // SUB_ROWS
KEY_TILE = 512
Q_TILE = 256
Q_SUBTILES = 4
KEY_TILES_PER_ITER = 16
MXU_TILE = 256
FF_CHUNK = MXU_TILE
N_FF_CHUNKS = D_FF // FF_CHUNK
V_ROWS = 2 * HEAD_DIM
NEG_BIG = -1e30
LOG2_E = 1.4426950408889634
SAFE_SHIFT_MAX = 50.0
SHIFT_MARGIN = 1.01
VMEM_LIMIT = 56 * 1024 * 1024

_F32 = jnp.float32
_BF16 = jnp.bfloat16


def _dot(a, b):
    return jnp.dot(a, b, preferred_element_type=_F32)


def _rms(x, g):
    return x * lax.rsqrt(jnp.mean(x * x, axis=-1, keepdims=True) + EPS) * g


def _gelu(x):
    return 0.5 * x * (1.0 + lax.erf(x * (2.0 ** -0.5)))


def _group_sumsq(t, ones_blockdiag):
    width = ones_blockdiag.shape[0]
    sq = t * t
    hi = sq.astype(_BF16)
    lo = (sq - hi.astype(_F32)).astype(_BF16)
    parts = [_dot(hi[:, o:o + width], ones_blockdiag) + _dot(lo[:, o:o + width], ones_blockdiag)
             for o in range(0, t.shape[1], width)]
    return parts[0] if len(parts) == 1 else jnp.concatenate(parts, axis=1)


def _rope(t, cos, sin_signed):
    width = t.shape[-1]
    lane = lax.broadcasted_iota(jnp.int32, t.shape, 1)
    first_half = (lane & (2 * ROPE_AXIS_FREQS - 1)) < ROPE_AXIS_FREQS
    partner = jnp.where(first_half, pltpu.roll(t, width - ROPE_AXIS_FREQS, 1), pltpu.roll(t, ROPE_AXIS_FREQS, 1))
    return t * cos + partner * sin_signed


def _sub_rows(r):
    return slice(r * SUB_ROWS, (r + 1) * SUB_ROWS)


def _swiglu_hidden(xn_bf16, wgu_ref, act_scr, rows):
    for c in range(N_FF_CHUNKS):
        g = _dot(xn_bf16, wgu_ref[:, c * FF_CHUNK:(c + 1) * FF_CHUNK])
        u = _dot(xn_bf16, wgu_ref[:, D_FF + c * FF_CHUNK:D_FF + (c + 1) * FF_CHUNK])
        act_scr[rows, c * FF_CHUNK:(c + 1) * FF_CHUNK] = (g * jax.nn.sigmoid(g) * u).astype(_BF16)


def _pre_kernel(x_ref, cos_ref, sin_ref, g1_ref, wgu_ref, wdn_ref, gmix_ref, win_ref, gq_ref, gk_ref, ggv_ref,
                wsp_ref, bsp_ref, wbg_ref, eq_ref, ek_ref,
                x1_ref, q_ref, k2_ref, vt_ref, kn_ref, sga_ref, mb_ref, act_scr):
    subs = [_sub_rows(r) for r in range(N_SUB)]
    x = [x_ref[rs, :] for rs in subs]
    xn = [_rms(xi, g1_ref[...]).astype(_BF16) for xi in x]
    for r, rs in enumerate(subs):
        _swiglu_hidden(xn[r], wgu_ref, act_scr, rs)
    x1 = [x[r] + 0.5 * _dot(act_scr[rs, :], wdn_ref[...]) for r, rs in enumerate(subs)]
    for r, rs in enumerate(subs):
        x1_ref[rs, :] = x1[r]
    h = [_rms(xi, gmix_ref[...]).astype(_BF16) for xi in x1]

    o_kv = ATTN_WIDTH
    o_gu = o_kv + 2 * KV_WIDTH
    o_gv = o_gu + GMLP_WIDTH
    o_ga = o_gv + GMLP_WIDTH
    o_gb = o_ga + D_MODEL
    q = [_dot(hr, win_ref[:, 0:o_kv]) for hr in h]
    kv = [_dot(hr, win_ref[:, o_kv:o_gu]) for hr in h]
    u = [_dot(hr, win_ref[:, o_gu:o_gv]) for hr in h]
    vg = [_dot(hr, win_ref[:, o_gv:o_ga]) for hr in h]
    for r, rs in enumerate(subs):
        sga_ref[rs, :] = jax.nn.sigmoid(_dot(h[r], win_ref[:, o_ga:o_gb])).astype(_BF16)
    q_ss = [_group_sumsq(qr, eq_ref[...]) for qr in q]
    k_ss = [_group_sumsq(kvr[:, :KV_WIDTH], ek_ref[...]) for kvr in kv]
    gate_b = [jax.nn.sigmoid(_dot(hr, win_ref[:, o_gb:o_gb + D_MODEL])) for hr in h]

    k = []
    for r, rs in enumerate(subs):
        cos = cos_ref[rs, :]
        sin = sin_ref[rs, :]
        qn = q[r] * lax.rsqrt(q_ss[r] * (1.0 / HEAD_DIM) + EPS) * gq_ref[...]
        qn = _rope(qn, jnp.concatenate([cos] * (ATTN_WIDTH // LANES), axis=1),
                   jnp.concatenate([sin] * (ATTN_WIDTH // LANES), axis=1))
        q_ref[rs, :] = (qn * (HEAD_DIM ** -0.5 * LOG2_E)).astype(_BF16)
        kn = kv[r][:, :KV_WIDTH] * lax.rsqrt(k_ss[r] * (1.0 / HEAD_DIM) + EPS) * gk_ref[...]
        k.append(_rope(kn, cos, sin))

    lane_c = lax.broadcasted_iota(jnp.int32, (CHUNK, LANES), 1)
    sg = []
    for r in range(N_SUB):
        vn = _rms(_gelu(vg[r]), ggv_ref[...])
        mixed_rows = []
        for ci in range(SUB_ROWS // CHUNK):
            cols = []
            for j in range(GMLP_WIDTH // LANES):
                vs = vn[ci * CHUNK:(ci + 1) * CHUNK, j * LANES:(j + 1) * LANES].astype(_BF16)
                zero = jnp.zeros_like(vs)
                rhs = jnp.concatenate([jnp.where(lane_c < GMLP_GROUP_DIM, vs, zero),
                                       jnp.where(lane_c < GMLP_GROUP_DIM, zero, vs)], axis=0)
                cols.append(_dot(wsp_ref[j], rhs))
            mixed_rows.append(jnp.concatenate(cols, axis=1) + bsp_ref[...])
        sg.append((_gelu(u[r]) * jnp.concatenate(mixed_rows, axis=0)).astype(_BF16))

    k_norm2 = [jnp.max(_group_sumsq(kr, ek_ref[...]), axis=0, keepdims=True) for kr in k]
    kn_ref[0] = jnp.broadcast_to(functools.reduce(jnp.maximum, k_norm2), kn_ref.shape[1:])
    for r, rs in enumerate(subs):
        mb_ref[rs, :] = (gate_b[r] * _dot(sg[r], wbg_ref[...])).astype(_BF16)

    ones = jnp.ones((V_ROWS - HEAD_DIM, SUB_ROWS), _BF16)
    lane = lax.broadcasted_iota(jnp.int32, (SUB_ROWS, LANES), 1)
    for r, rs in enumerate(subs):
        k_swapped = pltpu.roll(k[r], HEAD_DIM, 1)
        k2_ref[0, rs, :] = jnp.where(lane < HEAD_DIM, k[r], k_swapped).astype(_BF16)
        k2_ref[1, rs, :] = jnp.where(lane < HEAD_DIM, k_swapped, k[r]).astype(_BF16)
        vt = kv[r][:, KV_WIDTH:].T.astype(_BF16)
        t, lanes = divmod(r * SUB_ROWS, KEY_TILE)
        for kh in range(N_KV_HEADS):
            vt_ref[kh, t, 0:HEAD_DIM, lanes:lanes + SUB_ROWS] = vt[kh * HEAD_DIM:(kh + 1) * HEAD_DIM, :]
            vt_ref[kh, t, HEAD_DIM:V_ROWS, lanes:lanes + SUB_ROWS] = ones


def _attn_kernel(q_ref, k_ref, vt_ref, kn_ref, o_ref, qs_scr, c_scr, acc_scr, p_scr, *, n_key_tiles):
    tq = Q_TILE
    lane = lax.broadcasted_iota(jnp.int32, (tq, LANES), 1)
    for u in range(Q_SUBTILES):
        for hq in range(Q_PER_KV):
            pair = q_ref[u * tq:(u + 1) * tq, (hq // 2) * LANES:(hq // 2 + 1) * LANES]
            keep = (lane < HEAD_DIM) if hq % 2 == 0 else (lane >= HEAD_DIM)
            qs_scr[u, hq * tq:(hq + 1) * tq, :] = jnp.where(keep, pair, jnp.zeros_like(pair))
    acc_scr[...] = jnp.zeros(acc_scr.shape, _F32)

    def scores(u, kt, c):
        start = pl.multiple_of(kt * KEY_TILE, KEY_TILE)
        return lax.dot_general(k_ref[pl.ds(start, KEY_TILE), :], qs_scr[u, c * tq:(c + 1) * tq, :],
                               (((1,), (1,)), ((), ())), preferred_element_type=_F32)

    k_norm2 = jnp.max(kn_ref[...], axis=0)
    lane8 = lax.broadcasted_iota(jnp.int32, k_norm2.shape, 1)
    head_lo = pl.program_id(1) * HEAD_DIM
    this_head = (lane8 >= head_lo) & (lane8 < head_lo + HEAD_DIM)
    k_max2 = jnp.max(jnp.where(this_head, k_norm2, 0.0), axis=1, keepdims=True)
    largest_bound = None
    for u in range(Q_SUBTILES):
        qf = qs_scr[u].astype(_F32)
        q_norm2 = lax.dot_general(jnp.ones((8, LANES), _BF16), (qf * qf).astype(_BF16), (((1,), (1,)), ((), ())),
                                  preferred_element_type=_F32)
        bound = jnp.sqrt(q_norm2 * k_max2) * SHIFT_MARGIN
        c_scr[u] = bound[0:1, :]
        largest_bound = jnp.max(bound) if largest_bound is None else jnp.maximum(largest_bound, jnp.max(bound))

    def probabilities(u, kt, slot):
        for c in range(Q_PER_KV):
            cols = slice(c * tq, (c + 1) * tq)
            p_scr[slot, :, cols] = jnp.exp2(scores(u, kt, c) - c_scr[u, :, cols]).astype(_BF16)

    def accumulate(u, kt, slot):
        for c in range(Q_PER_KV):
            cols = slice(c * tq, (c + 1) * tq)
            acc_scr[u, :, cols] += _dot(vt_ref[kt], p_scr[slot, :, cols])

    probabilities(0, 0, 0)

    @pl.when(largest_bound > SAFE_SHIFT_MAX)
    def _():
        for u in range(Q_SUBTILES):
            def running_max(kt, m, u=u):
                tile_max = [jnp.max(scores(u, kt, c), axis=0, keepdims=True) for c in range(Q_PER_KV)]
                return jnp.maximum(m, jnp.concatenate(tile_max, axis=1))
            c_scr[u] = lax.fori_loop(0, n_key_tiles, running_max, jnp.full(c_scr.shape[1:], NEG_BIG, _F32))
        probabilities(0, 0, 0)

    unroll = min(KEY_TILES_PER_ITER, n_key_tiles)
    assert n_key_tiles % unroll == 0 and unroll % 2 == 0 and Q_SUBTILES >= 2
    groups_per_query_tile = n_key_tiles // unroll
    n_stages = Q_SUBTILES * n_key_tiles

    def key_tile_group(g, carry):
        u = lax.div(g, groups_per_query_tile)
        kt0 = lax.rem(g, groups_per_query_tile) * unroll
        for j in range(unroll):
            if j + 1 < unroll:
                u_next, kt_next = u, kt0 + j + 1
            else:
                stage = jnp.minimum((g + 1) * unroll, n_stages - 1)
                u_next, kt_next = lax.div(stage, n_key_tiles), lax.rem(stage, n_key_tiles)
            probabilities(u_next, kt_next, (j + 1) % 2)
            accumulate(u, kt0 + j, j % 2)
        return carry

    lax.fori_loop(0, Q_SUBTILES * groups_per_query_tile, key_tile_group, 0)

    for u in range(Q_SUBTILES):
        acc = acc_scr[u]
        out_t = acc[0:HEAD_DIM, :] / acc[HEAD_DIM:HEAD_DIM + 1, :]
        for j in range(Q_PER_KV // 2):
            pair_t = jnp.concatenate([out_t[:, (2 * j) * tq:(2 * j + 1) * tq],
                                      out_t[:, (2 * j + 1) * tq:(2 * j + 2) * tq]], axis=0)
            o_ref[u * tq:(u + 1) * tq, j * LANES:(j + 1) * LANES] = pair_t.T.astype(_BF16)


def _post_kernel(a_ref, sga_ref, mb_ref, x1_ref, p_ref, wba_ref, wout_ref, g2_ref, wgu_ref, wdn_ref, gple_ref,
                 wpg_ref, wple_ref, gfin_ref, y_ref, act_scr):
    subs = [_sub_rows(r) for r in range(N_SUB)]
    branch = [_dot(a_ref[rs, :], wba_ref[...]) for rs in subs]
    ple = [_dot(p_ref[rs, :].astype(_BF16), wple_ref[...]) for rs in subs]
    merged = [(sga_ref[rs, :].astype(_F32) * branch[r] + mb_ref[rs, :].astype(_F32)).astype(_BF16)
              for r, rs in enumerate(subs)]
    x2 = [x1_ref[rs, :] + _dot(merged[r], wout_ref[...]) for r, rs in enumerate(subs)]
    xn = [_rms(xi, g2_ref[...]).astype(_BF16) for xi in x2]
    for r, rs in enumerate(subs):
        _swiglu_hidden(xn[r], wgu_ref, act_scr, rs)
    x3 = [x2[r] + 0.5 * _dot(act_scr[rs, :], wdn_ref[...]) for r, rs in enumerate(subs)]
    gate = [jax.nn.sigmoid(_dot(_rms(xi, gple_ref[...]).astype(_BF16), wpg_ref[...])) for xi in x3]
    for r, rs in enumerate(subs):
        y_ref[rs, :] = _rms(x3[r] + gate[r] * ple[r], gfin_ref[...])


def _resident(shape):
    nd = len(shape)
    return pl.BlockSpec(shape, lambda *_: (0,) * nd, pipeline_mode=pl.Buffered(1))


def _rows(width, tile=ROW_TILE):
    return pl.BlockSpec((tile, width), lambda i: (i, 0))


def _rope_tables(seq):
    rows = seq // GRID_W
    row = jnp.repeat(jnp.arange(rows, dtype=_F32), GRID_W)
    col = jnp.tile(jnp.arange(GRID_W, dtype=_F32), rows)
    inv = jnp.power(jnp.float32(ROPE_THETA), -jnp.arange(ROPE_AXIS_FREQS, dtype=_F32) / ROPE_AXIS_FREQS)
    ang = jnp.stack([row[:, None] * inv, col[:, None] * inv], axis=1)
    cos = jnp.cos(ang)
    sin = jnp.sin(ang)
    cos_h = jnp.concatenate([cos[:, 0], cos[:, 0], cos[:, 1], cos[:, 1]], axis=1)
    sin_h = jnp.concatenate([-sin[:, 0], sin[:, 0], -sin[:, 1], sin[:, 1]], axis=1)
    return jnp.tile(cos_h, (1, LANES // HEAD_DIM)), jnp.tile(sin_h, (1, LANES // HEAD_DIM))


def _block_diag_ones(width):
    idx = jnp.arange(width) // HEAD_DIM
    return (idx[:, None] == idx[None, :]).astype(_BF16)


def _trunk(x, p, prm, cos_t, sin_t):
    batch, seq, _ = x.shape
    rows = batch * seq
    n_tiles = rows // ROW_TILE
    seq_tiles = seq // ROW_TILE
    x2d = x.reshape(rows, D_MODEL)
    p2d = p.reshape(rows, PLE_DIM)
    cparams = pltpu.CompilerParams(dimension_semantics=("arbitrary",), vmem_limit_bytes=VMEM_LIMIT)

    pos_spec = pl.BlockSpec((ROW_TILE, LANES), lambda i: (i % seq_tiles, 0))
    x1, q, k2, vt, kn, sga, mb = pl.pallas_call(
        _pre_kernel,
        grid=(n_tiles,),
        in_specs=[_rows(D_MODEL), pos_spec, pos_spec,
                  _resident((1, D_MODEL)), _resident((D_MODEL, 2 * D_FF)), _resident((D_FF, D_MODEL)),
                  _resident((1, D_MODEL)), _resident(prm["w_in"].shape),
                  _resident((1, ATTN_WIDTH)), _resident((1, KV_WIDTH)), _resident((1, GMLP_WIDTH)),
                  _resident(prm["w_sp"].shape), _resident((CHUNK, GMLP_WIDTH)), _resident((GMLP_WIDTH, D_MODEL)),
                  _resident((MXU_TILE, MXU_TILE)), _resident((KV_WIDTH, KV_WIDTH))],
        out_specs=[_rows(D_MODEL), _rows(ATTN_WIDTH),
                   pl.BlockSpec((N_KV_HEADS, ROW_TILE, LANES), lambda i: (0, i, 0)),
                   pl.BlockSpec((N_KV_HEADS, ROW_TILE // KEY_TILE, V_ROWS, KEY_TILE), lambda i: (0, i, 0, 0)),
                   pl.BlockSpec((1, 8, LANES), lambda i: (i, 0, 0)),
                   _rows(D_MODEL), _rows(D_MODEL)],
        out_shape=[jax.ShapeDtypeStruct((rows, D_MODEL), _F32),
                   jax.ShapeDtypeStruct((rows, ATTN_WIDTH), _BF16),
                   jax.ShapeDtypeStruct((N_KV_HEADS, rows, LANES), _BF16),
                   jax.ShapeDtypeStruct((N_KV_HEADS, rows // KEY_TILE, V_ROWS, KEY_TILE), _BF16),
                   jax.ShapeDtypeStruct((n_tiles, 8, LANES), _F32),
                   jax.ShapeDtypeStruct((rows, D_MODEL), _BF16),
                   jax.ShapeDtypeStruct((rows, D_MODEL), _BF16)],
        scratch_shapes=[pltpu.VMEM((ROW_TILE, D_FF), _BF16)],
        compiler_params=cparams,
        name="pre",
    )(x2d, cos_t, sin_t, prm["g_ffn1"], prm["w_gu1"], prm["w_dn1"], prm["g_mix"], prm["w_in"],
      prm["g_q"], prm["g_k"], prm["g_gv"], prm["w_sp"], prm["b_sp"], prm["w_bg"], prm["e_q"], prm["e_k"])

    q_tiles = seq // (Q_SUBTILES * Q_TILE)
    a = pl.pallas_call(
        functools.partial(_attn_kernel, n_key_tiles=seq // KEY_TILE),
        grid=(batch, N_KV_HEADS, q_tiles),
        in_specs=[pl.BlockSpec((Q_SUBTILES * Q_TILE, 2 * LANES), lambda b, kh, qi: (b * q_tiles + qi, kh)),
                  pl.BlockSpec((None, seq, LANES), lambda b, kh, qi: (kh, b, 0)),
                  pl.BlockSpec((None, seq // KEY_TILE, V_ROWS, KEY_TILE), lambda b, kh, qi: (kh, b, 0, 0)),
                  pl.BlockSpec((seq_tiles, 8, LANES), lambda b, kh, qi: (b, 0, 0))],
        out_specs=pl.BlockSpec((Q_SUBTILES * Q_TILE, 2 * LANES), lambda b, kh, qi: (b * q_tiles + qi, kh)),
        out_shape=jax.ShapeDtypeStruct((rows, ATTN_WIDTH), _BF16),
        scratch_shapes=[pltpu.VMEM((Q_SUBTILES, Q_PER_KV * Q_TILE, LANES), _BF16),
                        pltpu.VMEM((Q_SUBTILES, 1, Q_PER_KV * Q_TILE), _F32),
                        pltpu.VMEM((Q_SUBTILES, V_ROWS, Q_PER_KV * Q_TILE), _F32),
                        pltpu.VMEM((2, KEY_TILE, Q_PER_KV * Q_TILE), _BF16)],
        compiler_params=pltpu.CompilerParams(dimension_semantics=("arbitrary",) * 3, vmem_limit_bytes=VMEM_LIMIT),
        name="attn",
    )(q, k2, vt, kn)

    y = pl.pallas_call(
        _post_kernel,
        grid=(n_tiles,),
        in_specs=[_rows(ATTN_WIDTH), _rows(D_MODEL), _rows(D_MODEL), _rows(D_MODEL), _rows(PLE_DIM),
                  _resident((ATTN_WIDTH, D_MODEL)), _resident((D_MODEL, D_MODEL)),
                  _resident((1, D_MODEL)), _resident((D_MODEL, 2 * D_FF)), _resident((D_FF, D_MODEL)),
                  _resident((1, D_MODEL)), _resident((D_MODEL, D_MODEL)), _resident((PLE_DIM, D_MODEL)),
                  _resident((1, D_MODEL))],
        out_specs=_rows(D_MODEL),
        out_shape=jax.ShapeDtypeStruct((rows, D_MODEL), _F32),
        scratch_shapes=[pltpu.VMEM((ROW_TILE, D_FF), _BF16)],
        compiler_params=cparams,
        name="post",
    )(a, sga, mb, x1, p2d, prm["w_ba"], prm["w_out"], prm["g_ffn2"], prm["w_gu2"], prm["w_dn2"],
      prm["g_ple"], prm["w_pg"], prm["w_ple"], prm["g_final"])
    return y.reshape(batch, seq, D_MODEL)


def kernel(x_prompt, x_sample, p_prompt, p_sample, g_ffn1, w_ffn1_gu, w_ffn1_down, g_mix, w_in, g_q, g_k, g_gmlp_v, w_spatial, b_spatial, w_branch_attn, w_branch_gmlp, w_out, g_ffn2, w_ffn2_gu, w_ffn2_down, g_ple, w_ple_gate, w_ple, g_final):
    assert g_ffn1.shape[0] == 1, "the post kernel fuses the final norm into the single layer"
    cos_t, sin_t = _rope_tables(max(x_prompt.shape[1], x_sample.shape[1]))
    w_sp = w_spatial[0].reshape(GMLP_GROUPS // 2, 2, CHUNK, CHUNK).transpose(0, 2, 1, 3)
    prm = dict(
        g_ffn1=g_ffn1[0][None], w_gu1=w_ffn1_gu[0].astype(_BF16), w_dn1=w_ffn1_down[0].astype(_BF16),
        g_mix=g_mix[0][None], w_in=w_in[0].astype(_BF16),
        g_q=jnp.tile(g_q[0], N_HEADS)[None], g_k=jnp.tile(g_k[0], N_KV_HEADS)[None],
        g_gv=g_gmlp_v[0][None],
        w_sp=w_sp.reshape(GMLP_GROUPS // 2, CHUNK, 2 * CHUNK).astype(_BF16),
        b_sp=jnp.repeat(b_spatial[0].T, GMLP_GROUP_DIM, axis=1),
        w_bg=w_branch_gmlp[0].astype(_BF16), w_ba=w_branch_attn[0].astype(_BF16),
        w_out=w_out[0].astype(_BF16), g_ffn2=g_ffn2[0][None], w_gu2=w_ffn2_gu[0].astype(_BF16),
        w_dn2=w_ffn2_down[0].astype(_BF16), g_ple=g_ple[0][None], w_pg=w_ple_gate[0].astype(_BF16),
        w_ple=w_ple[0].astype(_BF16), g_final=g_final[None],
        e_q=_block_diag_ones(MXU_TILE), e_k=_block_diag_ones(KV_WIDTH),
    )
    return (_trunk(x_prompt, p_prompt[0], prm, cos_t, sin_t), _trunk(x_sample, p_sample[0], prm, cos_t, sin_t))
```

```python
import functools

import jax
import jax.numpy as jnp
from jax import lax
from jax.experimental import pallas as pl
from jax.experimental.pallas import tpu as pltpu

D_MODEL = 1024
N_HEADS = 8
N_KV_HEADS = 2
HEAD_DIM = 64
Q_PER_KV = N_HEADS // N_KV_HEADS
ATTN_WIDTH = N_HEADS * HEAD_DIM
KV_WIDTH = N_KV_HEADS * HEAD_DIM
GMLP_GROUPS = 8
GMLP_GROUP_DIM = 64
GMLP_WIDTH = GMLP_GROUPS * GMLP_GROUP_DIM
CHUNK = 128
GRID_W = 64
ROPE_THETA = 10000.0
ROPE_AXIS_FREQS = HEAD_DIM // 4
D_FF = 2816
PLE_DIM = 256
EPS = 1e-6

LANES = 128
ROW_TILE = 512
SUB_ROWS = 256
N_SUB = ROW_TILE // SUB_ROWS
KEY_TILE = 512
Q_TILE = 256
Q_SUBTILES = 8
KEY_TILES_PER_ITER = 16
MXU_TILE = 256
FF_CHUNK = MXU_TILE
N_FF_CHUNKS = D_FF // FF_CHUNK
V_ROWS = 2 * HEAD_DIM
NEG_BIG = -1e30
LOG2_E = 1.4426950408889634
SAFE_SHIFT_MAX = 50.0
SHIFT_MARGIN = 1.01
VMEM_LIMIT = 56 * 1024 * 1024

_F32 = jnp.float32
_BF16 = jnp.bfloat16


def _dot(a, b):
    return jnp.dot(a, b, preferred_element_type=_F32)


def _rms(x, g):
    return x * lax.rsqrt(jnp.mean(x * x, axis=-1, keepdims=True) + EPS) * g


def _gelu(x):
    return 0.5 * x * (1.0 + lax.erf(x * (2.0 ** -0.5)))


def _group_sumsq(t, ones_blockdiag):
    width = ones_blockdiag.shape[0]
    sq = t * t
    hi = sq.astype(_BF16)
    lo = (sq - hi.astype(_F32)).astype(_BF16)
    parts = [_dot(hi[:, o:o + width], ones_blockdiag) + _dot(lo[:, o:o + width], ones_blockdiag)
             for o in range(0, t.shape[1], width)]
    return parts[0] if len(parts) == 1 else jnp.concatenate(parts, axis=1)


def _rope(t, cos, sin_signed):
    width = t.shape[-1]
    lane = lax.broadcasted_iota(jnp.int32, t.shape, 1)
    first_half = (lane & (2 * ROPE_AXIS_FREQS - 1)) < ROPE_AXIS_FREQS
    partner = jnp.where(first_half, pltpu.roll(t, width - ROPE_AXIS_FREQS, 1), pltpu.roll(t, ROPE_AXIS_FREQS, 1))
    return t * cos + partner * sin_signed


def _sub_rows(r):
    return slice(r * SUB_ROWS, (r + 1) * SUB_ROWS)


def _swiglu_hidden(xn_bf16, wgu_ref, act_scr, rows):
    for c in range(N_FF_CHUNKS):
        g = _dot(xn_bf16, wgu_ref[:, c * FF_CHUNK:(c + 1) * FF_CHUNK])
        u = _dot(xn_bf16, wgu_ref[:, D_FF + c * FF_CHUNK:D_FF + (c + 1) * FF_CHUNK])
        act_scr[rows, c * FF_CHUNK:(c + 1) * FF_CHUNK] = (g * jax.nn.sigmoid(g) * u).astype(_BF16)


def _pre_kernel(x_ref, cos_ref, sin_ref, g1_ref, wgu_ref, wdn_ref, gmix_ref, win_ref, gq_ref, gk_ref, ggv_ref,
                wsp_ref, bsp_ref, wbg_ref, eq_ref, ek_ref,
                x1_ref, q_ref, k2_ref, vt_ref, sga_ref, mb_ref, act_scr):
    subs = [_sub_rows(r) for r in range(N_SUB)]
    x = [x_ref[rs, :] for rs in subs]
    xn = [_rms(xi, g1_ref[...]).astype(_BF16) for xi in x]
    for r, rs in enumerate(subs):
        _swiglu_hidden(xn[r], wgu_ref, act_scr, rs)
    x1 = [x[r] + 0.5 * _dot(act_scr[rs, :], wdn_ref[...]) for r, rs in enumerate(subs)]
    for r, rs in enumerate(subs):
        x1_ref[rs, :] = x1[r]
    h = [_rms(xi, gmix_ref[...]).astype(_BF16) for xi in x1]

    o_kv = ATTN_WIDTH
    o_gu = o_kv + 2 * KV_WIDTH
    o_gv = o_gu + GMLP_WIDTH
    o_ga = o_gv + GMLP_WIDTH
    o_gb = o_ga + D_MODEL
    q = [_dot(hr, win_ref[:, 0:o_kv]) for hr in h]
    kv = [_dot(hr, win_ref[:, o_kv:o_gu]) for hr in h]
    u = [_dot(hr, win_ref[:, o_gu:o_gv]) for hr in h]
    vg = [_dot(hr, win_ref[:, o_gv:o_ga]) for hr in h]
    for r, rs in enumerate(subs):
        sga_ref[rs, :] = jax.nn.sigmoid(_dot(h[r], win_ref[:, o_ga:o_gb])).astype(_BF16)
    q_ss = [_group_sumsq(qr, eq_ref[...]) for qr in q]
    k_ss = [_group_sumsq(kvr[:, :KV_WIDTH], ek_ref[...]) for kvr in kv]
    gate_b = [jax.nn.sigmoid(_dot(hr, win_ref[:, o_gb:o_gb + D_MODEL])) for hr in h]

    k = []
    for r, rs in enumerate(subs):
        cos = cos_ref[rs, :]
        sin = sin_ref[rs, :]
        qn = q[r] * lax.rsqrt(q_ss[r] * (1.0 / HEAD_DIM) + EPS) * gq_ref[...]
        qn = _rope(qn, jnp.concatenate([cos] * (ATTN_WIDTH // LANES), axis=1),
                   jnp.concatenate([sin] * (ATTN_WIDTH // LANES), axis=1))
        q_ref[rs, :] = (qn * (HEAD_DIM ** -0.5 * LOG2_E)).astype(_BF16)
        kn = kv[r][:, :KV_WIDTH] * lax.rsqrt(k_ss[r] * (1.0 / HEAD_DIM) + EPS) * gk_ref[...]
        k.append(_rope(kn, cos, sin))

    lane_c = lax.broadcasted_iota(jnp.int32, (CHUNK, LANES), 1)
    sg = []
    for r in range(N_SUB):
        vn = _rms(_gelu(vg[r]), ggv_ref[...])
        mixed_rows = []
        for ci in range(SUB_ROWS // CHUNK):
            cols = []
            for j in range(GMLP_WIDTH // LANES):
                vs = vn[ci * CHUNK:(ci + 1) * CHUNK, j * LANES:(j + 1) * LANES].astype(_BF16)
                zero = jnp.zeros_like(vs)
                rhs = jnp.concatenate([jnp.where(lane_c < GMLP_GROUP_DIM, vs, zero),
                                       jnp.where(lane_c < GMLP_GROUP_DIM, zero, vs)], axis=0)
                cols.append(_dot(wsp_ref[j], rhs))
            mixed_rows.append(jnp.concatenate(cols, axis=1) + bsp_ref[...])
        sg.append((_gelu(u[r]) * jnp.concatenate(mixed_rows, axis=0)).astype(_BF16))

    for r, rs in enumerate(subs):
        mb_ref[rs, :] = (gate_b[r] * _dot(sg[r], wbg_ref[...])).astype(_BF16)

    ones = jnp.ones((V_ROWS - HEAD_DIM, SUB_ROWS), _BF16)
    lane = lax.broadcasted_iota(jnp.int32, (SUB_ROWS, LANES), 1)
    for r, rs in enumerate(subs):
        k_swapped = pltpu.roll(k[r], HEAD_DIM, 1)
        k2_ref[0, rs, :] = jnp.where(lane < HEAD_DIM, k[r], k_swapped).astype(_BF16)
        k2_ref[1, rs, :] = jnp.where(lane < HEAD_DIM, k_swapped, k[r]).astype(_BF16)
        vt = kv[r][:, KV_WIDTH:].T.astype(_BF16)
        t, lanes = divmod(r * SUB_ROWS, KEY_TILE)
        for kh in range(N_KV_HEADS):
            vt_ref[kh, t, 0:HEAD_DIM, lanes:lanes + SUB_ROWS] = vt[kh * HEAD_DIM:(kh + 1) * HEAD_DIM, :]
            vt_ref[kh, t, HEAD_DIM:V_ROWS, lanes:lanes + SUB_ROWS] = ones


def _attn_kernel(bound_ref, q_ref, k_ref, vt_ref, o_ref, qs_scr, c_scr, acc_scr, p_scr, *, n_key_tiles):
    tq = Q_TILE
    lane = lax.broadcasted_iota(jnp.int32, (tq, LANES), 1)
    for u in range(Q_SUBTILES):
        for hq in range(Q_PER_KV):
            pair = q_ref[u * tq:(u + 1) * tq, (hq // 2) * LANES:(hq // 2 + 1) * LANES]
            keep = (lane < HEAD_DIM) if hq % 2 == 0 else (lane >= HEAD_DIM)
            qs_scr[u, hq * tq:(hq + 1) * tq, :] = jnp.where(keep, pair, jnp.zeros_like(pair))
    acc_scr[...] = jnp.zeros(acc_scr.shape, _F32)

    def scores(u, kt, c):
        start = pl.multiple_of(kt * KEY_TILE, KEY_TILE)
        return lax.dot_general(k_ref[pl.ds(start, KEY_TILE), :], qs_scr[u, c * tq:(c + 1) * tq, :],
                               (((1,), (1,)), ((), ())), preferred_element_type=_F32)

    bound = bound_ref[0]
    c_scr[...] = jnp.full(c_scr.shape, bound, _F32)

    def probabilities(u, kt, slot):
        for c in range(Q_PER_KV):
            cols = slice(c * tq, (c + 1) * tq)
            p_scr[slot, :, cols] = jnp.exp2(scores(u, kt, c) - c_scr[u, :, cols]).astype(_BF16)

    def accumulate(u, kt, slot):
        for c in range(Q_PER_KV):
            cols = slice(c * tq, (c + 1) * tq)
            acc_scr[u, :, cols] += _dot(vt_ref[kt], p_scr[slot, :, cols])

    probabilities(0, 0, 0)

    @pl.when(bound > SAFE_SHIFT_MAX)
    def _():
        for u in range(Q_SUBTILES):
            def running_max(kt, m, u=u):
                tile_max = [jnp.max(scores(u, kt, c), axis=0, keepdims=True) for c in range(Q_PER_KV)]
                return jnp.maximum(m, jnp.concatenate(tile_max, axis=1))
            c_scr[u] = lax.fori_loop(0, n_key_tiles, running_max, jnp.full(c_scr.shape[1:], NEG_BIG, _F32))
        probabilities(0, 0, 0)

    unroll = min(KEY_TILES_PER_ITER, n_key_tiles)
    assert n_key_tiles % unroll == 0 and unroll % 2 == 0 and Q_SUBTILES >= 2
    groups_per_query_tile = n_key_tiles // unroll
    n_stages = Q_SUBTILES * n_key_tiles

    def key_tile_group(g, carry):
        u = lax.div(g, groups_per_query_tile)
        kt0 = lax.rem(g, groups_per_query_tile) * unroll
        for j in range(unroll):
            if j + 1 < unroll:
                u_next, kt_next = u, kt0 + j + 1
            else:
                stage = jnp.minimum((g + 1) * unroll, n_stages - 1)
                u_next, kt_next = lax.div(stage, n_key_tiles), lax.rem(stage, n_key_tiles)
            probabilities(u_next, kt_next, (j + 1) % 2)
            accumulate(u, kt0 + j, j % 2)
        return carry

    lax.fori_loop(0, Q_SUBTILES * groups_per_query_tile, key_tile_group, 0)

    for u in range(Q_SUBTILES):
        acc = acc_scr[u]
        out_t = acc[0:HEAD_DIM, :] / acc[HEAD_DIM:HEAD_DIM + 1, :]
        for j in range(Q_PER_KV // 2):
            pair_t = jnp.concatenate([out_t[:, (2 * j) * tq:(2 * j + 1) * tq],
                                      out_t[:, (2 * j + 1) * tq:(2 * j + 2) * tq]], axis=0)
            o_ref[u * tq:(u + 1) * tq, j * LANES:(j + 1) * LANES] = pair_t.T.astype(_BF16)


def _post_kernel(a_ref, sga_ref, mb_ref, x1_ref, p_ref, wba_ref, wout_ref, g2_ref, wgu_ref, wdn_ref, gple_ref,
                 wpg_ref, wple_ref, gfin_ref, y_ref, act_scr):
    subs = [_sub_rows(r) for r in range(N_SUB)]
    branch = [_dot(a_ref[rs, :], wba_ref[...]) for rs in subs]
    ple = [_dot(p_ref[rs, :].astype(_BF16), wple_ref[...]) for rs in subs]
    merged = [(sga_ref[rs, :].astype(_F32) * branch[r] + mb_ref[rs, :].astype(_F32)).astype(_BF16)
              for r, rs in enumerate(subs)]
    x2 = [x1_ref[rs, :] + _dot(merged[r], wout_ref[...]) for r, rs in enumerate(subs)]
    xn = [_rms(xi, g2_ref[...]).astype(_BF16) for xi in x2]
    for r, rs in enumerate(subs):
        _swiglu_hidden(xn[r], wgu_ref, act_scr, rs)
    x3 = [x2[r] + 0.5 * _dot(act_scr[rs, :], wdn_ref[...]) for r, rs in enumerate(subs)]
    gate = [jax.nn.sigmoid(_dot(_rms(xi, gple_ref[...]).astype(_BF16), wpg_ref[...])) for xi in x3]
    for r, rs in enumerate(subs):
        y_ref[rs, :] = _rms(x3[r] + gate[r] * ple[r], gfin_ref[...])


def _resident(shape):
    nd = len(shape)
    return pl.BlockSpec(shape, lambda *_: (0,) * nd, pipeline_mode=pl.Buffered(1))


def _rows(width, tile=ROW_TILE):
    return pl.BlockSpec((tile, width), lambda i: (i, 0))


def _rope_tables(seq):
    rows = seq // GRID_W
    row = jnp.repeat(jnp.arange(rows, dtype=_F32), GRID_W)
    col = jnp.tile(jnp.arange(GRID_W, dtype=_F32), rows)
    inv = jnp.power(jnp.float32(ROPE_THETA), -jnp.arange(ROPE_AXIS_FREQS, dtype=_F32) / ROPE_AXIS_FREQS)
    ang = jnp.stack([row[:, None] * inv, col[:, None] * inv], axis=1)
    cos = jnp.cos(ang)
    sin = jnp.sin(ang)
    cos_h = jnp.concatenate([cos[:, 0], cos[:, 0], cos[:, 1], cos[:, 1]], axis=1)
    sin_h = jnp.concatenate([-sin[:, 0], sin[:, 0], -sin[:, 1], sin[:, 1]], axis=1)
    return jnp.tile(cos_h, (1, LANES // HEAD_DIM)), jnp.tile(sin_h, (1, LANES // HEAD_DIM))


def _block_diag_ones(width):
    idx = jnp.arange(width) // HEAD_DIM
    return (idx[:, None] == idx[None, :]).astype(_BF16)


def _trunk(x, p, prm, cos_t, sin_t):
    batch, seq, _ = x.shape
    rows = batch * seq
    n_tiles = rows // ROW_TILE
    seq_tiles = seq // ROW_TILE
    x2d = x.reshape(rows, D_MODEL)
    p2d = p.reshape(rows, PLE_DIM)
    cparams = pltpu.CompilerParams(dimension_semantics=("arbitrary",), vmem_limit_bytes=VMEM_LIMIT)

    pos_spec = pl.BlockSpec((ROW_TILE, LANES), lambda i: (i % seq_tiles, 0))
    x1, q, k2, vt, sga, mb = pl.pallas_call(
        _pre_kernel,
        grid=(n_tiles,),
        in_specs=[_rows(D_MODEL), pos_spec, pos_spec,
                  _resident((1, D_MODEL)), _resident((D_MODEL, 2 * D_FF)), _resident((D_FF, D_MODEL)),
                  _resident((1, D_MODEL)), _resident(prm["w_in"].shape),
                  _resident((1, ATTN_WIDTH)), _resident((1, KV_WIDTH)), _resident((1, GMLP_WIDTH)),
                  _resident(prm["w_sp"].shape), _resident((CHUNK, GMLP_WIDTH)), _resident((GMLP_WIDTH, D_MODEL)),
                  _resident((MXU_TILE, MXU_TILE)), _resident((KV_WIDTH, KV_WIDTH))],
        out_specs=[_rows(D_MODEL), _rows(ATTN_WIDTH),
                   pl.BlockSpec((N_KV_HEADS, ROW_TILE, LANES), lambda i: (0, i, 0)),
                   pl.BlockSpec((N_KV_HEADS, ROW_TILE // KEY_TILE, V_ROWS, KEY_TILE), lambda i: (0, i, 0, 0)),
                   _rows(D_MODEL), _rows(D_MODEL)],
        out_shape=[jax.ShapeDtypeStruct((rows, D_MODEL), _F32),
                   jax.ShapeDtypeStruct((rows, ATTN_WIDTH), _BF16),
                   jax.ShapeDtypeStruct((N_KV_HEADS, rows, LANES), _BF16),
                   jax.ShapeDtypeStruct((N_KV_HEADS, rows // KEY_TILE, V_ROWS, KEY_TILE), _BF16),
                   jax.ShapeDtypeStruct((rows, D_MODEL), _BF16),
                   jax.ShapeDtypeStruct((rows, D_MODEL), _BF16)],
        scratch_shapes=[pltpu.VMEM((ROW_TILE, D_FF), _BF16)],
        compiler_params=cparams,
        name="pre",
    )(x2d, cos_t, sin_t, prm["g_ffn1"], prm["w_gu1"], prm["w_dn1"], prm["g_mix"], prm["w_in"],
      prm["g_q"], prm["g_k"], prm["g_gv"], prm["w_sp"], prm["b_sp"], prm["w_bg"], prm["e_q"], prm["e_k"])

    q_tiles = seq // (Q_SUBTILES * Q_TILE)
    a = pl.pallas_call(
        functools.partial(_attn_kernel, n_key_tiles=seq // KEY_TILE),
        grid=(batch, N_KV_HEADS, q_tiles),
        in_specs=[pl.BlockSpec(memory_space=pltpu.SMEM),
                  pl.BlockSpec((Q_SUBTILES * Q_TILE, 2 * LANES), lambda b, kh, qi: (b * q_tiles + qi, kh)),
                  pl.BlockSpec((None, seq, LANES), lambda b, kh, qi: (kh, b, 0)),
                  pl.BlockSpec((None, seq // KEY_TILE, V_ROWS, KEY_TILE), lambda b, kh, qi: (kh, b, 0, 0))],
        out_specs=pl.BlockSpec((Q_SUBTILES * Q_TILE, 2 * LANES), lambda b, kh, qi: (b * q_tiles + qi, kh)),
        out_shape=jax.ShapeDtypeStruct((rows, ATTN_WIDTH), _BF16),
        scratch_shapes=[pltpu.VMEM((Q_SUBTILES, Q_PER_KV * Q_TILE, LANES), _BF16),
                        pltpu.VMEM((Q_SUBTILES, 1, Q_PER_KV * Q_TILE), _F32),
                        pltpu.VMEM((Q_SUBTILES, V_ROWS, Q_PER_KV * Q_TILE), _F32),
                        pltpu.VMEM((2, KEY_TILE, Q_PER_KV * Q_TILE), _BF16)],
        compiler_params=pltpu.CompilerParams(dimension_semantics=("arbitrary",) * 3, vmem_limit_bytes=VMEM_LIMIT),
        name="attn",
    )(prm["score_bound"], q, k2, vt)

    y = pl.pallas_call(
        _post_kernel,
        grid=(n_tiles,),
        in_specs=[_rows(ATTN_WIDTH), _rows(D_MODEL), _rows(D_MODEL), _rows(D_MODEL), _rows(PLE_DIM),
                  _resident((ATTN_WIDTH, D_MODEL)), _resident((D_MODEL, D_MODEL)),
                  _resident((1, D_MODEL)), _resident((D_MODEL, 2 * D_FF)), _resident((D_FF, D_MODEL)),
                  _resident((1, D_MODEL)), _resident((D_MODEL, D_MODEL)), _resident((PLE_DIM, D_MODEL)),
                  _resident((1, D_MODEL))],
        out_specs=_rows(D_MODEL),
        out_shape=jax.ShapeDtypeStruct((rows, D_MODEL), _F32),
        scratch_shapes=[pltpu.VMEM((ROW_TILE, D_FF), _BF16)],
        compiler_params=cparams,
        name="post",
    )(a, sga, mb, x1, p2d, prm["w_ba"], prm["w_out"], prm["g_ffn2"], prm["w_gu2"], prm["w_dn2"],
      prm["g_ple"], prm["w_pg"], prm["w_ple"], prm["g_final"])
    return y.reshape(batch, seq, D_MODEL)


def kernel(x_prompt, x_sample, p_prompt, p_sample, g_ffn1, w_ffn1_gu, w_ffn1_down, g_mix, w_in, g_q, g_k, g_gmlp_v, w_spatial, b_spatial, w_branch_attn, w_branch_gmlp, w_out, g_ffn2, w_ffn2_gu, w_ffn2_down, g_ple, w_ple_gate, w_ple, g_final):
    assert g_ffn1.shape[0] == 1, "the post kernel fuses the final norm into the single layer"
    cos_t, sin_t = _rope_tables(max(x_prompt.shape[1], x_sample.shape[1]))
    w_sp = w_spatial[0].reshape(GMLP_GROUPS // 2, 2, CHUNK, CHUNK).transpose(0, 2, 1, 3)
    prm = dict(
        g_ffn1=g_ffn1[0][None], w_gu1=w_ffn1_gu[0].astype(_BF16), w_dn1=w_ffn1_down[0].astype(_BF16),
        g_mix=g_mix[0][None], w_in=w_in[0].astype(_BF16),
        g_q=jnp.tile(g_q[0], N_HEADS)[None], g_k=jnp.tile(g_k[0], N_KV_HEADS)[None],
        g_gv=g_gmlp_v[0][None],
        w_sp=w_sp.reshape(GMLP_GROUPS // 2, CHUNK, 2 * CHUNK).astype(_BF16),
        b_sp=jnp.repeat(b_spatial[0].T, GMLP_GROUP_DIM, axis=1),
        w_bg=w_branch_gmlp[0].astype(_BF16), w_ba=w_branch_attn[0].astype(_BF16),
        w_out=w_out[0].astype(_BF16), g_ffn2=g_ffn2[0][None], w_gu2=w_ffn2_gu[0].astype(_BF16),
        w_dn2=w_ffn2_down[0].astype(_BF16), g_ple=g_ple[0][None], w_pg=w_ple_gate[0].astype(_BF16),
        w_ple=w_ple[0].astype(_BF16), g_final=g_final[None],
        e_q=_block_diag_ones(MXU_TILE), e_k=_block_diag_ones(KV_WIDTH),
        score_bound=(jnp.max(jnp.abs(g_q[0])) * jnp.max(jnp.abs(g_k[0]))
                     * (HEAD_DIM ** 0.5 * LOG2_E * SHIFT_MARGIN)).reshape(1).astype(_F32),
    )
    return (_trunk(x_prompt, p_prompt[0], prm, cos_t, sin_t), _trunk(x_sample, p_sample[0], prm, cos_t, sin_t))
```

```python
import functools

import jax
import jax.numpy as jnp
from jax import lax
from jax.experimental import pallas as pl
from jax.experimental.pallas import tpu as pltpu

D_MODEL = 1024
N_HEADS = 8
N_KV_HEADS = 2
HEAD_DIM = 64
Q_PER_KV = N_HEADS // N_KV_HEADS
ATTN_WIDTH = N_HEADS * HEAD_DIM
KV_WIDTH = N_KV_HEADS * HEAD_DIM
GMLP_GROUPS = 8
GMLP_GROUP_DIM = 64
GMLP_WIDTH = GMLP_GROUPS * GMLP_GROUP_DIM
CHUNK = 128
GRID_W = 64
ROPE_THETA = 10000.0
ROPE_AXIS_FREQS = HEAD_DIM // 4
D_FF = 2816
PLE_DIM = 256
EPS = 1e-6

LANES = 128
ROW_TILE = 512
SUB_ROWS = 256
N_SUB = ROW_TILE // SUB_ROWS
KEY_TILE = 512
Q_TILE = 256
Q_SUBTILES = 8
KEY_TILES_PER_ITER = 32
MXU_TILE = 256
FF_CHUNK = MXU_TILE
N_FF_CHUNKS = D_FF // FF_CHUNK
V_ROWS = 2 * HEAD_DIM
NEG_BIG = -1e30
LOG2_E = 1.4426950408889634
SAFE_SHIFT_MAX = 50.0
SHIFT_MARGIN = 1.01
VMEM_LIMIT = 56 * 1024 * 1024

_F32 = jnp.float32
_BF16 = jnp.bfloat16


def _dot(a, b):
    return jnp.dot(a, b, preferred_element_type=_F32)


def _rms(x, g):
    return x * lax.rsqrt(jnp.mean(x * x, axis=-1, keepdims=True) + EPS) * g


def _gelu(x):
    return 0.5 * x * (1.0 + lax.erf(x * (2.0 ** -0.5)))


def _group_sumsq(t, ones_blockdiag):
    width = ones_blockdiag.shape[0]
    sq = t * t
    hi = sq.astype(_BF16)
    lo = (sq - hi.astype(_F32)).astype(_BF16)
    parts = [_dot(hi[:, o:o + width], ones_blockdiag) + _dot(lo[:, o:o + width], ones_blockdiag)
             for o in range(0, t.shape[1], width)]
    return parts[0] if len(parts) == 1 else jnp.concatenate(parts, axis=1)


def _rope(t, cos, sin_signed):
    width = t.shape[-1]
    lane = lax.broadcasted_iota(jnp.int32, t.shape, 1)
    first_half = (lane & (2 * ROPE_AXIS_FREQS - 1)) < ROPE_AXIS_FREQS
    partner = jnp.where(first_half, pltpu.roll(t, width - ROPE_AXIS_FREQS, 1), pltpu.roll(t, ROPE_AXIS_FREQS, 1))
    return t * cos + partner * sin_signed


def _sub_rows(r):
    return slice(r * SUB_ROWS, (r + 1) * SUB_ROWS)


def _swiglu_hidden(xn_bf16, wgu_ref, act_scr, rows):
    for c in range(N_FF_CHUNKS):
        g = _dot(xn_bf16, wgu_ref[:, c * FF_CHUNK:(c + 1) * FF_CHUNK])
        u = _dot(xn_bf16, wgu_ref[:, D_FF + c * FF_CHUNK:D_FF + (c + 1) * FF_CHUNK])
        act_scr[rows, c * FF_CHUNK:(c + 1) * FF_CHUNK] = (g * jax.nn.sigmoid(g) * u).astype(_BF16)


def _pre_kernel(x_ref, cos_ref, sin_ref, g1_ref, wgu_ref, wdn_ref, gmix_ref, win_ref, gq_ref, gk_ref, ggv_ref,
                wsp_ref, bsp_ref, wbg_ref, eq_ref, ek_ref,
                x1_ref, q_ref, k2_ref, vt_ref, sga_ref, mb_ref, act_scr):
    subs = [_sub_rows(r) for r in range(N_SUB)]
    x = [x_ref[rs, :] for rs in subs]
    xn = [_rms(xi, g1_ref[...]).astype(_BF16) for xi in x]
    for r, rs in enumerate(subs):
        _swiglu_hidden(xn[r], wgu_ref, act_scr, rs)
    x1 = [x[r] + 0.5 * _dot(act_scr[rs, :], wdn_ref[...]) for r, rs in enumerate(subs)]
    for r, rs in enumerate(subs):
        x1_ref[rs, :] = x1[r]
    h = [_rms(xi, gmix_ref[...]).astype(_BF16) for xi in x1]

    o_kv = ATTN_WIDTH
    o_gu = o_kv + 2 * KV_WIDTH
    o_gv = o_gu + GMLP_WIDTH
    o_ga = o_gv + GMLP_WIDTH
    o_gb = o_ga + D_MODEL
    q = [_dot(hr, win_ref[:, 0:o_kv]) for hr in h]
    kv = [_dot(hr, win_ref[:, o_kv:o_gu]) for hr in h]
    u = [_dot(hr, win_ref[:, o_gu:o_gv]) for hr in h]
    vg = [_dot(hr, win_ref[:, o_gv:o_ga]) for hr in h]
    for r, rs in enumerate(subs):
        sga_ref[rs, :] = jax.nn.sigmoid(_dot(h[r], win_ref[:, o_ga:o_gb])).astype(_BF16)
    q_ss = [_group_sumsq(qr, eq_ref[...]) for qr in q]
    k_ss = [_group_sumsq(kvr[:, :KV_WIDTH], ek_ref[...]) for kvr in kv]
    gate_b = [jax.nn.sigmoid(_dot(hr, win_ref[:, o_gb:o_gb + D_MODEL])) for hr in h]

    k = []
    for r, rs in enumerate(subs):
        cos = cos_ref[rs, :]
        sin = sin_ref[rs, :]
        qn = q[r] * lax.rsqrt(q_ss[r] * (1.0 / HEAD_DIM) + EPS) * gq_ref[...]
        qn = _rope(qn, jnp.concatenate([cos] * (ATTN_WIDTH // LANES), axis=1),
                   jnp.concatenate([sin] * (ATTN_WIDTH // LANES), axis=1))
        q_ref[rs, :] = (qn * (HEAD_DIM ** -0.5 * LOG2_E)).astype(_BF16)
        kn = kv[r][:, :KV_WIDTH] * lax.rsqrt(k_ss[r] * (1.0 / HEAD_DIM) + EPS) * gk_ref[...]
        k.append(_rope(kn, cos, sin))

    lane_c = lax.broadcasted_iota(jnp.int32, (CHUNK, LANES), 1)
    sg = []
    for r in range(N_SUB):
        vn = _rms(_gelu(vg[r]), ggv_ref[...])
        mixed_rows = []
        for ci in range(SUB_ROWS // CHUNK):
            cols = []
            for j in range(GMLP_WIDTH // LANES):
                vs = vn[ci * CHUNK:(ci + 1) * CHUNK, j * LANES:(j + 1) * LANES].astype(_BF16)
                zero = jnp.zeros_like(vs)
                rhs = jnp.concatenate([jnp.where(lane_c < GMLP_GROUP_DIM, vs, zero),
                                       jnp.where(lane_c < GMLP_GROUP_DIM, zero, vs)], axis=0)
                cols.append(_dot(wsp_ref[j], rhs))
            mixed_rows.append(jnp.concatenate(cols, axis=1) + bsp_ref[...])
        sg.append((_gelu(u[r]) * jnp.concatenate(mixed_rows, axis=0)).astype(_BF16))

    for r, rs in enumerate(subs):
        mb_ref[rs, :] = (gate_b[r] * _dot(sg[r], wbg_ref[...])).astype(_BF16)

    ones = jnp.ones((V_ROWS - HEAD_DIM, SUB_ROWS), _BF16)
    lane = lax.broadcasted_iota(jnp.int32, (SUB_ROWS, LANES), 1)
    for r, rs in enumerate(subs):
        k_swapped = pltpu.roll(k[r], HEAD_DIM, 1)
        k2_ref[0, rs, :] = jnp.where(lane < HEAD_DIM, k[r], k_swapped).astype(_BF16)
        k2_ref[1, rs, :] = jnp.where(lane < HEAD_DIM, k_swapped, k[r]).astype(_BF16)
        vt = kv[r][:, KV_WIDTH:].T.astype(_BF16)
        t, lanes = divmod(r * SUB_ROWS, KEY_TILE)
        for kh in range(N_KV_HEADS):
            vt_ref[kh, t, 0:HEAD_DIM, lanes:lanes + SUB_ROWS] = vt[kh * HEAD_DIM:(kh + 1) * HEAD_DIM, :]
            vt_ref[kh, t, HEAD_DIM:V_ROWS, lanes:lanes + SUB_ROWS] = ones


def _attn_kernel(bound_ref, q_ref, k_ref, vt_ref, o_ref, qs_scr, c_scr, acc_scr, p_scr, *, n_key_tiles):
    tq = Q_TILE
    lane = lax.broadcasted_iota(jnp.int32, (tq, LANES), 1)
    for u in range(Q_SUBTILES):
        for hq in range(Q_PER_KV):
            pair = q_ref[u * tq:(u + 1) * tq, (hq // 2) * LANES:(hq // 2 + 1) * LANES]
            keep = (lane < HEAD_DIM) if hq % 2 == 0 else (lane >= HEAD_DIM)
            qs_scr[u, hq * tq:(hq + 1) * tq, :] = jnp.where(keep, pair, jnp.zeros_like(pair))
    acc_scr[...] = jnp.zeros(acc_scr.shape, _F32)

    def scores(u, kt, c):
        start = pl.multiple_of(kt * KEY_TILE, KEY_TILE)
        return lax.dot_general(k_ref[pl.ds(start, KEY_TILE), :], qs_scr[u, c * tq:(c + 1) * tq, :],
                               (((1,), (1,)), ((), ())), preferred_element_type=_F32)

    bound = bound_ref[0]
    c_scr[...] = jnp.full(c_scr.shape, bound, _F32)

    def probabilities(u, kt, slot):
        for c in range(Q_PER_KV):
            cols = slice(c * tq, (c + 1) * tq)
            p_scr[slot, :, cols] = jnp.exp2(scores(u, kt, c) - c_scr[u, :, cols]).astype(_BF16)

    def accumulate(u, kt, slot):
        for c in range(Q_PER_KV):
            cols = slice(c * tq, (c + 1) * tq)
            acc_scr[u, :, cols] += _dot(vt_ref[kt], p_scr[slot, :, cols])

    probabilities(0, 0, 0)

    @pl.when(bound > SAFE_SHIFT_MAX)
    def _():
        for u in range(Q_SUBTILES):
            def running_max(kt, m, u=u):
                tile_max = [jnp.max(scores(u, kt, c), axis=0, keepdims=True) for c in range(Q_PER_KV)]
                return jnp.maximum(m, jnp.concatenate(tile_max, axis=1))
            c_scr[u] = lax.fori_loop(0, n_key_tiles, running_max, jnp.full(c_scr.shape[1:], NEG_BIG, _F32))
        probabilities(0, 0, 0)

    unroll = min(KEY_TILES_PER_ITER, n_key_tiles)
    assert n_key_tiles % unroll == 0 and unroll % 2 == 0 and Q_SUBTILES >= 2
    groups_per_query_tile = n_key_tiles // unroll
    n_stages = Q_SUBTILES * n_key_tiles

    def key_tile_group(g, carry):
        u = lax.div(g, groups_per_query_tile)
        kt0 = lax.rem(g, groups_per_query_tile) * unroll
        for j in range(unroll):
            if j + 1 < unroll:
                u_next, kt_next = u, kt0 + j + 1
            else:
                stage = jnp.minimum((g + 1) * unroll, n_stages - 1)
                u_next, kt_next = lax.div(stage, n_key_tiles), lax.rem(stage, n_key_tiles)
            probabilities(u_next, kt_next, (j + 1) % 2)
            accumulate(u, kt0 + j, j % 2)
        return carry

    lax.fori_loop(0, Q_SUBTILES * groups_per_query_tile, key_tile_group, 0)

    for u in range(Q_SUBTILES):
        acc = acc_scr[u]
        out_t = acc[0:HEAD_DIM, :] / acc[HEAD_DIM:HEAD_DIM + 1, :]
        for j in range(Q_PER_KV // 2):
            pair_t = jnp.concatenate([out_t[:, (2 * j) * tq:(2 * j + 1) * tq],
                                      out_t[:, (2 * j + 1) * tq:(2 * j + 2) * tq]], axis=0)
            o_ref[u * tq:(u + 1) * tq, j * LANES:(j + 1) * LANES] = pair_t.T.astype(_BF16)


def _post_kernel(a_ref, sga_ref, mb_ref, x1_ref, p_ref, wba_ref, wout_ref, g2_ref, wgu_ref, wdn_ref, gple_ref,
                 wpg_ref, wple_ref, gfin_ref, y_ref, act_scr):
    subs = [_sub_rows(r) for r in range(N_SUB)]
    branch = [_dot(a_ref[rs, :], wba_ref[...]) for rs in subs]
    ple = [_dot(p_ref[rs, :].astype(_BF16), wple_ref[...]) for rs in subs]
    merged = [(sga_ref[rs, :].astype(_F32) * branch[r] + mb_ref[rs, :].astype(_F32)).astype(_BF16)
              for r, rs in enumerate(subs)]
    x2 = [x1_ref[rs, :] + _dot(merged[r], wout_ref[...]) for r, rs in enumerate(subs)]
    xn = [_rms(xi, g2_ref[...]).astype(_BF16) for xi in x2]
    for r, rs in enumerate(subs):
        _swiglu_hidden(xn[r], wgu_ref, act_scr, rs)
    x3 = [x2[r] + 0.5 * _dot(act_scr[rs, :], wdn_ref[...]) for r, rs in enumerate(subs)]
    gate = [jax.nn.sigmoid(_dot(_rms(xi, gple_ref[...]).astype(_BF16), wpg_ref[...])) for xi in x3]
    for r, rs in enumerate(subs):
        y_ref[rs, :] = _rms(x3[r] + gate[r] * ple[r], gfin_ref[...])


def _resident(shape):
    nd = len(shape)
    return pl.BlockSpec(shape, lambda *_: (0,) * nd, pipeline_mode=pl.Buffered(1))


def _rows(width, tile=ROW_TILE):
    return pl.BlockSpec((tile, width), lambda i: (i, 0))


def _rope_tables(seq):
    rows = seq // GRID_W
    row = jnp.repeat(jnp.arange(rows, dtype=_F32), GRID_W)
    col = jnp.tile(jnp.arange(GRID_W, dtype=_F32), rows)
    inv = jnp.power(jnp.float32(ROPE_THETA), -jnp.arange(ROPE_AXIS_FREQS, dtype=_F32) / ROPE_AXIS_FREQS)
    ang = jnp.stack([row[:, None] * inv, col[:, None] * inv], axis=1)
    cos = jnp.cos(ang)
    sin = jnp.sin(ang)
    cos_h = jnp.concatenate([cos[:, 0], cos[:, 0], cos[:, 1], cos[:, 1]], axis=1)
    sin_h = jnp.concatenate([-sin[:, 0], sin[:, 0], -sin[:, 1], sin[:, 1]], axis=1)
    return jnp.tile(cos_h, (1, LANES // HEAD_DIM)), jnp.tile(sin_h, (1, LANES // HEAD_DIM))


def _block_diag_ones(width):
    idx = jnp.arange(width) // HEAD_DIM
    return (idx[:, None] == idx[None, :]).astype(_BF16)


def _trunk(x, p, prm, cos_t, sin_t):
    batch, seq, _ = x.shape
    rows = batch * seq
    n_tiles = rows // ROW_TILE
    seq_tiles = seq // ROW_TILE
    x2d = x.reshape(rows, D_MODEL)
    p2d = p.reshape(rows, PLE_DIM)
    cparams = pltpu.CompilerParams(dimension_semantics=("arbitrary",), vmem_limit_bytes=VMEM_LIMIT)

    pos_spec = pl.BlockSpec((ROW_TILE, LANES), lambda i: (i % seq_tiles, 0))
    x1, q, k2, vt, sga, mb = pl.pallas_call(
        _pre_kernel,
        grid=(n_tiles,),
        in_specs=[_rows(D_MODEL), pos_spec, pos_spec,
                  _resident((1, D_MODEL)), _resident((D_MODEL, 2 * D_FF)), _resident((D_FF, D_MODEL)),
                  _resident((1, D_MODEL)), _resident(prm["w_in"].shape),
                  _resident((1, ATTN_WIDTH)), _resident((1, KV_WIDTH)), _resident((1, GMLP_WIDTH)),
                  _resident(prm["w_sp"].shape), _resident((CHUNK, GMLP_WIDTH)), _resident((GMLP_WIDTH, D_MODEL)),
                  _resident((MXU_TILE, MXU_TILE)), _resident((KV_WIDTH, KV_WIDTH))],
        out_specs=[_rows(D_MODEL), _rows(ATTN_WIDTH),
                   pl.BlockSpec((N_KV_HEADS, ROW_TILE, LANES), lambda i: (0, i, 0)),
                   pl.BlockSpec((N_KV_HEADS, ROW_TILE // KEY_TILE, V_ROWS, KEY_TILE), lambda i: (0, i, 0, 0)),
                   _rows(D_MODEL), _rows(D_MODEL)],
        out_shape=[jax.ShapeDtypeStruct((rows, D_MODEL), _F32),
                   jax.ShapeDtypeStruct((rows, ATTN_WIDTH), _BF16),
                   jax.ShapeDtypeStruct((N_KV_HEADS, rows, LANES), _BF16),
                   jax.ShapeDtypeStruct((N_KV_HEADS, rows // KEY_TILE, V_ROWS, KEY_TILE), _BF16),
                   jax.ShapeDtypeStruct((rows, D_MODEL), _BF16),
                   jax.ShapeDtypeStruct((rows, D_MODEL), _BF16)],
        scratch_shapes=[pltpu.VMEM((ROW_TILE, D_FF), _BF16)],
        compiler_params=cparams,
        name="pre",
    )(x2d, cos_t, sin_t, prm["g_ffn1"], prm["w_gu1"], prm["w_dn1"], prm["g_mix"], prm["w_in"],
      prm["g_q"], prm["g_k"], prm["g_gv"], prm["w_sp"], prm["b_sp"], prm["w_bg"], prm["e_q"], prm["e_k"])

    q_tiles = seq // (Q_SUBTILES * Q_TILE)
    a = pl.pallas_call(
        functools.partial(_attn_kernel, n_key_tiles=seq // KEY_TILE),
        grid=(batch, N_KV_HEADS, q_tiles),
        in_specs=[pl.BlockSpec(memory_space=pltpu.SMEM),
                  pl.BlockSpec((Q_SUBTILES * Q_TILE, 2 * LANES), lambda b, kh, qi: (b * q_tiles + qi, kh)),
                  pl.BlockSpec((None, seq, LANES), lambda b, kh, qi: (kh, b, 0)),
                  pl.BlockSpec((None, seq // KEY_TILE, V_ROWS, KEY_TILE), lambda b, kh, qi: (kh, b, 0, 0))],
        out_specs=pl.BlockSpec((Q_SUBTILES * Q_TILE, 2 * LANES), lambda b, kh, qi: (b * q_tiles + qi, kh)),
        out_shape=jax.ShapeDtypeStruct((rows, ATTN_WIDTH), _BF16),
        scratch_shapes=[pltpu.VMEM((Q_SUBTILES, Q_PER_KV * Q_TILE, LANES), _BF16),
                        pltpu.VMEM((Q_SUBTILES, 1, Q_PER_KV * Q_TILE), _F32),
                        pltpu.VMEM((Q_SUBTILES, V_ROWS, Q_PER_KV * Q_TILE), _F32),
                        pltpu.VMEM((2, KEY_TILE, Q_PER_KV * Q_TILE), _BF16)],
        compiler_params=pltpu.CompilerParams(dimension_semantics=("arbitrary",) * 3, vmem_limit_bytes=VMEM_LIMIT),
        name="attn",
    )(prm["score_bound"], q, k2, vt)

    y = pl.pallas_call(
        _post_kernel,
        grid=(n_tiles,),
        in_specs=[_rows(ATTN_WIDTH), _rows(D_MODEL), _rows(D_MODEL), _rows(D_MODEL), _rows(PLE_DIM),
                  _resident((ATTN_WIDTH, D_MODEL)), _resident((D_MODEL, D_MODEL)),
                  _resident((1, D_MODEL)), _resident((D_MODEL, 2 * D_FF)), _resident((D_FF, D_MODEL)),
                  _resident((1, D_MODEL)), _resident((D_MODEL, D_MODEL)), _resident((PLE_DIM, D_MODEL)),
                  _resident((1, D_MODEL))],
        out_specs=_rows(D_MODEL),
        out_shape=jax.ShapeDtypeStruct((rows, D_MODEL), _F32),
        scratch_shapes=[pltpu.VMEM((ROW_TILE, D_FF), _BF16)],
        compiler_params=cparams,
        name="post",
    )(a, sga, mb, x1, p2d, prm["w_ba"], prm["w_out"], prm["g_ffn2"], prm["w_gu2"], prm["w_dn2"],
      prm["g_ple"], prm["w_pg"], prm["w_ple"], prm["g_final"])
    return y.reshape(batch, seq, D_MODEL)


def kernel(x_prompt, x_sample, p_prompt, p_sample, g_ffn1, w_ffn1_gu, w_ffn1_down, g_mix, w_in, g_q, g_k, g_gmlp_v, w_spatial, b_spatial, w_branch_attn, w_branch_gmlp, w_out, g_ffn2, w_ffn2_gu, w_ffn2_down, g_ple, w_ple_gate, w_ple, g_final):
    assert g_ffn1.shape[0] == 1, "the post kernel fuses the final norm into the single layer"
    cos_t, sin_t = _rope_tables(max(x_prompt.shape[1], x_sample.shape[1]))
    w_sp = w_spatial[0].reshape(GMLP_GROUPS // 2, 2, CHUNK, CHUNK).transpose(0, 2, 1, 3)
    prm = dict(
        g_ffn1=g_ffn1[0][None], w_gu1=w_ffn1_gu[0].astype(_BF16), w_dn1=w_ffn1_down[0].astype(_BF16),
        g_mix=g_mix[0][None], w_in=w_in[0].astype(_BF16),
        g_q=jnp.tile(g_q[0], N_HEADS)[None], g_k=jnp.tile(g_k[0], N_KV_HEADS)[None],
        g_gv=g_gmlp_v[0][None],
        w_sp=w_sp.reshape(GMLP_GROUPS // 2, CHUNK, 2 * CHUNK).astype(_BF16),
        b_sp=jnp.repeat(b_spatial[0].T, GMLP_GROUP_DIM, axis=1),
        w_bg=w_branch_gmlp[0].astype(_BF16), w_ba=w_branch_attn[0].astype(_BF16),
        w_out=w_out[0].astype(_BF16), g_ffn2=g_ffn2[0][None], w_gu2=w_ffn2_gu[0].astype(_BF16),
        w_dn2=w_ffn2_down[0].astype(_BF16), g_ple=g_ple[0][None], w_pg=w_ple_gate[0].astype(_BF16),
        w_ple=w_ple[0].astype(_BF16), g_final=g_final[None],
        e_q=_block_diag_ones(MXU_TILE), e_k=_block_diag_ones(KV_WIDTH),
        score_bound=(jnp.max(jnp.abs(g_q[0])) * jnp.max(jnp.abs(g_k[0]))
                     * (HEAD_DIM ** 0.5 * LOG2_E * SHIFT_MARGIN)).reshape(1).astype(_F32),
    )
    return (_trunk(x_prompt, p_prompt[0], prm, cos_t, sin_t), _trunk(x_sample, p_sample[0], prm, cos_t, sin_t))
```

```python
import functools

import jax
import jax.numpy as jnp
from jax import lax
from jax.experimental import pallas as pl
from jax.experimental.pallas import tpu as pltpu

D_MODEL = 1024
N_HEADS = 8
N_KV_HEADS = 2
HEAD_DIM = 64
Q_PER_KV = N_HEADS // N_KV_HEADS
ATTN_WIDTH = N_HEADS * HEAD_DIM
KV_WIDTH = N_KV_HEADS * HEAD_DIM
GMLP_GROUPS = 8
GMLP_GROUP_DIM = 64
GMLP_WIDTH = GMLP_GROUPS * GMLP_GROUP_DIM
CHUNK = 128
GRID_W = 64
ROPE_THETA = 10000.0
ROPE_AXIS_FREQS = HEAD_DIM // 4
D_FF = 2816
PLE_DIM = 256
EPS = 1e-6

LANES = 128
ROW_TILE = 512
SUB_ROWS = 256
N_SUB = ROW_TILE // SUB_ROWS
KEY_TILE = 512
Q_TILE = 256
Q_SUBTILES = 8
KEY_TILES_PER_ITER = 16
MXU_TILE = 256
FF_CHUNK = MXU_TILE
N_FF_CHUNKS = D_FF // FF_CHUNK
V_ROWS = 2 * HEAD_DIM
NEG_BIG = -1e30
LOG2_E = 1.4426950408889634
SAFE_SHIFT_MAX = 50.0
SHIFT_MARGIN = 1.01
VMEM_LIMIT = 56 * 1024 * 1024

_F32 = jnp.float32
_BF16 = jnp.bfloat16


def _dot(a, b):
    return jnp.dot(a, b, preferred_element_type=_F32)


def _rms(x, g):
    return x * lax.rsqrt(jnp.mean(x * x, axis=-1, keepdims=True) + EPS) * g


def _gelu(x):
    return 0.5 * x * (1.0 + lax.erf(x * (2.0 ** -0.5)))


def _group_sumsq(t, ones_blockdiag):
    width = ones_blockdiag.shape[0]
    sq = (t * t).astype(_BF16)
    parts = [_dot(sq[:, o:o + width], ones_blockdiag) for o in range(0, t.shape[1], width)]
    return parts[0] if len(parts) == 1 else jnp.concatenate(parts, axis=1)


def _rope(t, cos, sin_signed):
    width = t.shape[-1]
    lane = lax.broadcasted_iota(jnp.int32, t.shape, 1)
    first_half = (lane & (2 * ROPE_AXIS_FREQS - 1)) < ROPE_AXIS_FREQS
    partner = jnp.where(first_half, pltpu.roll(t, width - ROPE_AXIS_FREQS, 1), pltpu.roll(t, ROPE_AXIS_FREQS, 1))
    return t * cos + partner * sin_signed


def _sub_rows(r):
    return slice(r * SUB_ROWS, (r + 1) * SUB_ROWS)


def _swiglu_hidden(xn_bf16, wgu_ref, act_scr, rows):
    for c in range(N_FF_CHUNKS):
        g = _dot(xn_bf16, wgu_ref[:, c * FF_CHUNK:(c + 1) * FF_CHUNK])
        u = _dot(xn_bf16, wgu_ref[:, D_FF + c * FF_CHUNK:D_FF + (c + 1) * FF_CHUNK])
        act_scr[rows, c * FF_CHUNK:(c + 1) * FF_CHUNK] = (g * jax.nn.sigmoid(g) * u).astype(_BF16)


def _pre_kernel(x_ref, cos_ref, sin_ref, g1_ref, wgu_ref, wdn_ref, gmix_ref, win_ref, gq_ref, gk_ref, ggv_ref,
                wsp_ref, bsp_ref, wbg_ref, eq_ref, ek_ref,
                x1_ref, q_ref, k2_ref, vt_ref, sga_ref, mb_ref, act_scr):
    subs = [_sub_rows(r) for r in range(N_SUB)]
    x = [x_ref[rs, :] for rs in subs]
    xn = [_rms(xi, g1_ref[...]).astype(_BF16) for xi in x]
    for r, rs in enumerate(subs):
        _swiglu_hidden(xn[r], wgu_ref, act_scr, rs)
    x1 = [x[r] + 0.5 * _dot(act_scr[rs, :], wdn_ref[...]) for r, rs in enumerate(subs)]
    for r, rs in enumerate(subs):
        x1_ref[rs, :] = x1[r]
    h = [_rms(xi, gmix_ref[...]).astype(_BF16) for xi in x1]

    o_kv = ATTN_WIDTH
    o_gu = o_kv + 2 * KV_WIDTH
    o_gv = o_gu + GMLP_WIDTH
    o_ga = o_gv + GMLP_WIDTH
    o_gb = o_ga + D_MODEL
    q = [_dot(hr, win_ref[:, 0:o_kv]) for hr in h]
    kv = [_dot(hr, win_ref[:, o_kv:o_gu]) for hr in h]
    u = [_dot(hr, win_ref[:, o_gu:o_gv]) for hr in h]
    vg = [_dot(hr, win_ref[:, o_gv:o_ga]) for hr in h]
    for r, rs in enumerate(subs):
        sga_ref[rs, :] = jax.nn.sigmoid(_dot(h[r], win_ref[:, o_ga:o_gb])).astype(_BF16)
    q_ss = [_group_sumsq(qr, eq_ref[...]) for qr in q]
    k_ss = [_group_sumsq(kvr[:, :KV_WIDTH], ek_ref[...]) for kvr in kv]
    gate_b = [jax.nn.sigmoid(_dot(hr, win_ref[:, o_gb:o_gb + D_MODEL])) for hr in h]

    k = []
    for r, rs in enumerate(subs):
        cos = cos_ref[rs, :]
        sin = sin_ref[rs, :]
        qn = q[r] * lax.rsqrt(q_ss[r] * (1.0 / HEAD_DIM) + EPS) * gq_ref[...]
        qn = _rope(qn, jnp.concatenate([cos] * (ATTN_WIDTH // LANES), axis=1),
                   jnp.concatenate([sin] * (ATTN_WIDTH // LANES), axis=1))
        q_ref[rs, :] = (qn * (HEAD_DIM ** -0.5 * LOG2_E)).astype(_BF16)
        kn = kv[r][:, :KV_WIDTH] * lax.rsqrt(k_ss[r] * (1.0 / HEAD_DIM) + EPS) * gk_ref[...]
        k.append(_rope(kn, cos, sin))

    lane_c = lax.broadcasted_iota(jnp.int32, (CHUNK, LANES), 1)
    sg = []
    for r in range(N_SUB):
        vn = _rms(_gelu(vg[r]), ggv_ref[...])
        mixed_rows = []
        for ci in range(SUB_ROWS // CHUNK):
            cols = []
            for j in range(GMLP_WIDTH // LANES):
                vs = vn[ci * CHUNK:(ci + 1) * CHUNK, j * LANES:(j + 1) * LANES].astype(_BF16)
                zero = jnp.zeros_like(vs)
                rhs = jnp.concatenate([jnp.where(lane_c < GMLP_GROUP_DIM, vs, zero),
                                       jnp.where(lane_c < GMLP_GROUP_DIM, zero, vs)], axis=0)
                cols.append(_dot(wsp_ref[j], rhs))
            mixed_rows.append(jnp.concatenate(cols, axis=1) + bsp_ref[...])
        sg.append((_gelu(u[r]) * jnp.concatenate(mixed_rows, axis=0)).astype(_BF16))

    for r, rs in enumerate(subs):
        mb_ref[rs, :] = (gate_b[r] * _dot(sg[r], wbg_ref[...])).astype(_BF16)

    ones = jnp.ones((V_ROWS - HEAD_DIM, SUB_ROWS), _BF16)
    lane = lax.broadcasted_iota(jnp.int32, (SUB_ROWS, LANES), 1)
    for r, rs in enumerate(subs):
        k_swapped = pltpu.roll(k[r], HEAD_DIM, 1)
        k2_ref[0, rs, :] = jnp.where(lane < HEAD_DIM, k[r], k_swapped).astype(_BF16)
        k2_ref[1, rs, :] = jnp.where(lane < HEAD_DIM, k_swapped, k[r]).astype(_BF16)
        vt = kv[r][:, KV_WIDTH:].T.astype(_BF16)
        t, lanes = divmod(r * SUB_ROWS, KEY_TILE)
        for kh in range(N_KV_HEADS):
            vt_ref[kh, t, 0:HEAD_DIM, lanes:lanes + SUB_ROWS] = vt[kh * HEAD_DIM:(kh + 1) * HEAD_DIM, :]
            vt_ref[kh, t, HEAD_DIM:V_ROWS, lanes:lanes + SUB_ROWS] = ones


def _attn_kernel(bound_ref, q_ref, k_ref, vt_ref, o_ref, qs_scr, c_scr, acc_scr, p_scr, *, n_key_tiles):
    tq = Q_TILE
    lane = lax.broadcasted_iota(jnp.int32, (tq, LANES), 1)
    for u in range(Q_SUBTILES):
        for hq in range(Q_PER_KV):
            pair = q_ref[u * tq:(u + 1) * tq, (hq // 2) * LANES:(hq // 2 + 1) * LANES]
            keep = (lane < HEAD_DIM) if hq % 2 == 0 else (lane >= HEAD_DIM)
            qs_scr[u, hq * tq:(hq + 1) * tq, :] = jnp.where(keep, pair, jnp.zeros_like(pair))
    acc_scr[...] = jnp.zeros(acc_scr.shape, _F32)

    def scores(u, kt, c):
        start = pl.multiple_of(kt * KEY_TILE, KEY_TILE)
        return lax.dot_general(k_ref[pl.ds(start, KEY_TILE), :], qs_scr[u, c * tq:(c + 1) * tq, :],
                               (((1,), (1,)), ((), ())), preferred_element_type=_F32)

    bound = bound_ref[0]
    c_scr[...] = jnp.full(c_scr.shape, bound, _F32)

    def probabilities(u, kt, slot):
        for c in range(Q_PER_KV):
            cols = slice(c * tq, (c + 1) * tq)
            p_scr[slot, :, cols] = jnp.exp2(scores(u, kt, c) - c_scr[u, :, cols]).astype(_BF16)

    def accumulate(u, kt, slot):
        for c in range(Q_PER_KV):
            cols = slice(c * tq, (c + 1) * tq)
            acc_scr[u, :, cols] += _dot(vt_ref[kt], p_scr[slot, :, cols])

    probabilities(0, 0, 0)

    @pl.when(bound > SAFE_SHIFT_MAX)
    def _():
        for u in range(Q_SUBTILES):
            def running_max(kt, m, u=u):
                tile_max = [jnp.max(scores(u, kt, c), axis=0, keepdims=True) for c in range(Q_PER_KV)]
                return jnp.maximum(m, jnp.concatenate(tile_max, axis=1))
            c_scr[u] = lax.fori_loop(0, n_key_tiles, running_max, jnp.full(c_scr.shape[1:], NEG_BIG, _F32))
        probabilities(0, 0, 0)

    unroll = min(KEY_TILES_PER_ITER, n_key_tiles)
    assert n_key_tiles % unroll == 0 and unroll % 2 == 0 and Q_SUBTILES >= 2
    groups_per_query_tile = n_key_tiles // unroll
    n_stages = Q_SUBTILES * n_key_tiles

    def key_tile_group(g, carry):
        u = lax.div(g, groups_per_query_tile)
        kt0 = lax.rem(g, groups_per_query_tile) * unroll
        for j in range(unroll):
            if j + 1 < unroll:
                u_next, kt_next = u, kt0 + j + 1
            else:
                stage = jnp.minimum((g + 1) * unroll, n_stages - 1)
                u_next, kt_next = lax.div(stage, n_key_tiles), lax.rem(stage, n_key_tiles)
            probabilities(u_next, kt_next, (j + 1) % 2)
            accumulate(u, kt0 + j, j % 2)
        return carry

    lax.fori_loop(0, Q_SUBTILES * groups_per_query_tile, key_tile_group, 0)

    for u in range(Q_SUBTILES):
        acc = acc_scr[u]
        out_t = acc[0:HEAD_DIM, :] / acc[HEAD_DIM:HEAD_DIM + 1, :]
        for j in range(Q_PER_KV // 2):
            pair_t = jnp.concatenate([out_t[:, (2 * j) * tq:(2 * j + 1) * tq],
                                      out_t[:, (2 * j + 1) * tq:(2 * j + 2) * tq]], axis=0)
            o_ref[u * tq:(u + 1) * tq, j * LANES:(j + 1) * LANES] = pair_t.T.astype(_BF16)


def _post_kernel(a_ref, sga_ref, mb_ref, x1_ref, p_ref, wba_ref, wout_ref, g2_ref, wgu_ref, wdn_ref, gple_ref,
                 wpg_ref, wple_ref, gfin_ref, y_ref, act_scr):
    subs = [_sub_rows(r) for r in range(N_SUB)]
    branch = [_dot(a_ref[rs, :], wba_ref[...]) for rs in subs]
    ple = [_dot(p_ref[rs, :].astype(_BF16), wple_ref[...]) for rs in subs]
    merged = [(sga_ref[rs, :].astype(_F32) * branch[r] + mb_ref[rs, :].astype(_F32)).astype(_BF16)
              for r, rs in enumerate(subs)]
    x2 = [x1_ref[rs, :] + _dot(merged[r], wout_ref[...]) for r, rs in enumerate(subs)]
    xn = [_rms(xi, g2_ref[...]).astype(_BF16) for xi in x2]
    for r, rs in enumerate(subs):
        _swiglu_hidden(xn[r], wgu_ref, act_scr, rs)
    x3 = [x2[r] + 0.5 * _dot(act_scr[rs, :], wdn_ref[...]) for r, rs in enumerate(subs)]
    gate = [jax.nn.sigmoid(_dot(_rms(xi, gple_ref[...]).astype(_BF16), wpg_ref[...])) for xi in x3]
    for r, rs in enumerate(subs):
        y_ref[rs, :] = _rms(x3[r] + gate[r] * ple[r], gfin_ref[...])


def _resident(shape):
    nd = len(shape)
    return pl.BlockSpec(shape, lambda *_: (0,) * nd, pipeline_mode=pl.Buffered(1))


def _rows(width, tile=ROW_TILE):
    return pl.BlockSpec((tile, width), lambda i: (i, 0))


def _rope_tables(seq):
    rows = seq // GRID_W
    row = jnp.repeat(jnp.arange(rows, dtype=_F32), GRID_W)
    col = jnp.tile(jnp.arange(GRID_W, dtype=_F32), rows)
    inv = jnp.power(jnp.float32(ROPE_THETA), -jnp.arange(ROPE_AXIS_FREQS, dtype=_F32) / ROPE_AXIS_FREQS)
    ang = jnp.stack([row[:, None] * inv, col[:, None] * inv], axis=1)
    cos = jnp.cos(ang)
    sin = jnp.sin(ang)
    cos_h = jnp.concatenate([cos[:, 0], cos[:, 0], cos[:, 1], cos[:, 1]], axis=1)
    sin_h = jnp.concatenate([-sin[:, 0], sin[:, 0], -sin[:, 1], sin[:, 1]], axis=1)
    return jnp.tile(cos_h, (1, LANES // HEAD_DIM)), jnp.tile(sin_h, (1, LANES // HEAD_DIM))


def _block_diag_ones(width):
    idx = jnp.arange(width) // HEAD_DIM
    return (idx[:, None] == idx[None, :]).astype(_BF16)


def _trunk(x, p, prm, cos_t, sin_t):
    batch, seq, _ = x.shape
    rows = batch * seq
    n_tiles = rows // ROW_TILE
    seq_tiles = seq // ROW_TILE
    x2d = x.reshape(rows, D_MODEL)
    p2d = p.reshape(rows, PLE_DIM)
    cparams = pltpu.CompilerParams(dimension_semantics=("arbitrary",), vmem_limit_bytes=VMEM_LIMIT)

    pos_spec = pl.BlockSpec((ROW_TILE, LANES), lambda i: (i % seq_tiles, 0))
    x1, q, k2, vt, sga, mb = pl.pallas_call(
        _pre_kernel,
        grid=(n_tiles,),
        in_specs=[_rows(D_MODEL), pos_spec, pos_spec,
                  _resident((1, D_MODEL)), _resident((D_MODEL, 2 * D_FF)), _resident((D_FF, D_MODEL)),
                  _resident((1, D_MODEL)), _resident(prm["w_in"].shape),
                  _resident((1, ATTN_WIDTH)), _resident((1, KV_WIDTH)), _resident((1, GMLP_WIDTH)),
                  _resident(prm["w_sp"].shape), _resident((CHUNK, GMLP_WIDTH)), _resident((GMLP_WIDTH, D_MODEL)),
                  _resident((MXU_TILE, MXU_TILE)), _resident((KV_WIDTH, KV_WIDTH))],
        out_specs=[_rows(D_MODEL), _rows(ATTN_WIDTH),
                   pl.BlockSpec((N_KV_HEADS, ROW_TILE, LANES), lambda i: (0, i, 0)),
                   pl.BlockSpec((N_KV_HEADS, ROW_TILE // KEY_TILE, V_ROWS, KEY_TILE), lambda i: (0, i, 0, 0)),
                   _rows(D_MODEL), _rows(D_MODEL)],
        out_shape=[jax.ShapeDtypeStruct((rows, D_MODEL), _F32),
                   jax.ShapeDtypeStruct((rows, ATTN_WIDTH), _BF16),
                   jax.ShapeDtypeStruct((N_KV_HEADS, rows, LANES), _BF16),
                   jax.ShapeDtypeStruct((N_KV_HEADS, rows // KEY_TILE, V_ROWS, KEY_TILE), _BF16),
                   jax.ShapeDtypeStruct((rows, D_MODEL), _BF16),
                   jax.ShapeDtypeStruct((rows, D_MODEL), _BF16)],
        scratch_shapes=[pltpu.VMEM((ROW_TILE, D_FF), _BF16)],
        compiler_params=cparams,
        name="pre",
    )(x2d, cos_t, sin_t, prm["g_ffn1"], prm["w_gu1"], prm["w_dn1"], prm["g_mix"], prm["w_in"],
      prm["g_q"], prm["g_k"], prm["g_gv"], prm["w_sp"], prm["b_sp"], prm["w_bg"], prm["e_q"], prm["e_k"])

    q_tiles = seq // (Q_SUBTILES * Q_TILE)
    a = pl.pallas_call(
        functools.partial(_attn_kernel, n_key_tiles=seq // KEY_TILE),
        grid=(batch, N_KV_HEADS, q_tiles),
        in_specs=[pl.BlockSpec(memory_space=pltpu.SMEM),
                  pl.BlockSpec((Q_SUBTILES * Q_TILE, 2 * LANES), lambda b, kh, qi: (b * q_tiles + qi, kh)),
                  pl.BlockSpec((None, seq, LANES), lambda b, kh, qi: (kh, b, 0)),
                  pl.BlockSpec((None, seq // KEY_TILE, V_ROWS, KEY_TILE), lambda b, kh, qi: (kh, b, 0, 0))],
        out_specs=pl.BlockSpec((Q_SUBTILES * Q_TILE, 2 * LANES), lambda b, kh, qi: (b * q_tiles + qi, kh)),
        out_shape=jax.ShapeDtypeStruct((rows, ATTN_WIDTH), _BF16),
        scratch_shapes=[pltpu.VMEM((Q_SUBTILES, Q_PER_KV * Q_TILE, LANES), _BF16),
                        pltpu.VMEM((Q_SUBTILES, 1, Q_PER_KV * Q_TILE), _F32),
                        pltpu.VMEM((Q_SUBTILES, V_ROWS, Q_PER_KV * Q_TILE), _F32),
                        pltpu.VMEM((2, KEY_TILE, Q_PER_KV * Q_TILE), _BF16)],
        compiler_params=pltpu.CompilerParams(dimension_semantics=("arbitrary",) * 3, vmem_limit_bytes=VMEM_LIMIT),
        name="attn",
    )(prm["score_bound"], q, k2, vt)

    y = pl.pallas_call(
        _post_kernel,
        grid=(n_tiles,),
        in_specs=[_rows(ATTN_WIDTH), _rows(D_MODEL), _rows(D_MODEL), _rows(D_MODEL), _rows(PLE_DIM),
                  _resident((ATTN_WIDTH, D_MODEL)), _resident((D_MODEL, D_MODEL)),
                  _resident((1, D_MODEL)), _resident((D_MODEL, 2 * D_FF)), _resident((D_FF, D_MODEL)),
                  _resident((1, D_MODEL)), _resident((D_MODEL, D_MODEL)), _resident((PLE_DIM, D_MODEL)),
                  _resident((1, D_MODEL))],
        out_specs=_rows(D_MODEL),
        out_shape=jax.ShapeDtypeStruct((rows, D_MODEL), _F32),
        scratch_shapes=[pltpu.VMEM((ROW_TILE, D_FF), _BF16)],
        compiler_params=cparams,
        name="post",
    )(a, sga, mb, x1, p2d, prm["w_ba"], prm["w_out"], prm["g_ffn2"], prm["w_gu2"], prm["w_dn2"],
      prm["g_ple"], prm["w_pg"], prm["w_ple"], prm["g_final"])
    return y.reshape(batch, seq, D_MODEL)


def kernel(x_prompt, x_sample, p_prompt, p_sample, g_ffn1, w_ffn1_gu, w_ffn1_down, g_mix, w_in, g_q, g_k, g_gmlp_v, w_spatial, b_spatial, w_branch_attn, w_branch_gmlp, w_out, g_ffn2, w_ffn2_gu, w_ffn2_down, g_ple, w_ple_gate, w_ple, g_final):
    assert g_ffn1.shape[0] == 1, "the post kernel fuses the final norm into the single layer"
    cos_t, sin_t = _rope_tables(max(x_prompt.shape[1], x_sample.shape[1]))
    w_sp = w_spatial[0].reshape(GMLP_GROUPS // 2, 2, CHUNK, CHUNK).transpose(0, 2, 1, 3)
    prm = dict(
        g_ffn1=g_ffn1[0][None], w_gu1=w_ffn1_gu[0].astype(_BF16), w_dn1=w_ffn1_down[0].astype(_BF16),
        g_mix=g_mix[0][None], w_in=w_in[0].astype(_BF16),
        g_q=jnp.tile(g_q[0], N_HEADS)[None], g_k=jnp.tile(g_k[0], N_KV_HEADS)[None],
        g_gv=g_gmlp_v[0][None],
        w_sp=w_sp.reshape(GMLP_GROUPS // 2, CHUNK, 2 * CHUNK).astype(_BF16),
        b_sp=jnp.repeat(b_spatial[0].T, GMLP_GROUP_DIM, axis=1),
        w_bg=w_branch_gmlp[0].astype(_BF16), w_ba=w_branch_attn[0].astype(_BF16),
        w_out=w_out[0].astype(_BF16), g_ffn2=g_ffn2[0][None], w_gu2=w_ffn2_gu[0].astype(_BF16),
        w_dn2=w_ffn2_down[0].astype(_BF16), g_ple=g_ple[0][None], w_pg=w_ple_gate[0].astype(_BF16),
        w_ple=w_ple[0].astype(_BF16), g_final=g_final[None],
        e_q=_block_diag_ones(MXU_TILE), e_k=_block_diag_ones(KV_WIDTH),
        score_bound=(jnp.max(jnp.abs(g_q[0])) * jnp.max(jnp.abs(g_k[0]))
                     * (HEAD_DIM ** 0.5 * LOG2_E * SHIFT_MARGIN)).reshape(1).astype(_F32),
    )
    return (_trunk(x_prompt, p_prompt[0], prm, cos_t, sin_t), _trunk(x_sample, p_sample[0], prm, cos_t, sin_t))
```
